```python
import math
import jax
import jax.numpy as jnp
from jax import lax
import numpy as np

D_MODEL = 1024
BATCH = 32
SEQ = 2048
DEPTH = 1
DEC_BATCH = 128
DEC_SEQ = 4
PAST_LEN = 8192
PAGE_SIZE = 128

SSM_WIDTH = D_MODEL // 2
ATTN_WIDTH = D_MODEL - SSM_WIDTH
MIX_WIDTH = SSM_WIDTH + ATTN_WIDTH
SSM_GROUP = 16
N_SSM_GROUPS = SSM_WIDTH // SSM_GROUP
SSM_STATE = 64
HEAD_DIM = 64
N_HEADS = ATTN_WIDTH // HEAD_DIM
IN_COLS = 2 * SSM_WIDTH + 4 * ATTN_WIDTH
Q_BLOCK = 128
ATTN_SCALE = HEAD_DIM ** -0.5
SB_BIAS_INIT = -6.0
SB_BIAS_NOISE = 0.5
RMS_EPS = 1e-6
DT_MIN = 1e-3
DT_MAX = 1e-1
POOL_NUM = 5
POOL_DEN = 4
STATE_SCALE = 0.1

kernel_name = 'hymba_s5_stickbreaking_step'


def rmsnorm(x, gain):
    xf = x.astype(jnp.float32)
    y = xf * lax.rsqrt(jnp.mean(xf * xf, axis=-1, keepdims=True) + RMS_EPS)
    return (y * gain.astype(jnp.float32)).astype(x.dtype)


def in_projection(x, gain, w_in):
    b, t, _ = x.shape
    p = rmsnorm(x, gain) @ w_in
    cuts = [SSM_WIDTH, 2 * SSM_WIDTH, 2 * SSM_WIDTH + ATTN_WIDTH,
            2 * SSM_WIDTH + 2 * ATTN_WIDTH, 2 * SSM_WIDTH + 3 * ATTN_WIDTH]
    u, z_ssm, q, k, v, z_attn = jnp.split(p, cuts, axis=-1)
    heads = lambda a: a.reshape(b, t, N_HEADS, HEAD_DIM)
    return u, z_ssm, heads(q), heads(k), heads(v), z_attn


def out_projection(y_ssm, z_ssm, y_attn, z_attn, w_out):
    mix = jnp.concatenate([y_ssm * jax.nn.silu(z_ssm), y_attn * jax.nn.silu(z_attn)], axis=-1)
    return mix @ w_out


def zoh_discretize(lam_re, lam_im, log_dt, b_re, b_im):
    lr = lam_re.astype(jnp.float32)
    li = lam_im.astype(jnp.float32)
    dt = jnp.exp(log_dt.astype(jnp.float32))[:, None]
    mag = jnp.exp(lr * dt)
    a_re = mag * jnp.cos(li * dt)
    a_im = mag * jnp.sin(li * dt)
    den = lr * lr + li * li
    g_re = ((a_re - 1.0) * lr + a_im * li) / den
    g_im = (a_im * lr - (a_re - 1.0) * li) / den
    br = b_re.astype(jnp.float32)
    bi = b_im.astype(jnp.float32)
    bb_re = g_re[..., None] * br - g_im[..., None] * bi
    bb_im = g_re[..., None] * bi + g_im[..., None] * br
    return a_re, a_im, bb_re, bb_im


def _complex_linear_combine(e1, e2):
    a1r, a1i, b1r, b1i = e1
    a2r, a2i, b2r, b2i = e2
    return (a2r * a1r - a2i * a1i,
            a2r * a1i + a2i * a1r,
            a2r * b1r - a2i * b1i + b2r,
            a2r * b1i + a2i * b1r + b2i)


def s5_branch(u, h0_re, h0_im, a_re, a_im, bb_re, bb_im, c_re, c_im, d_skip, w_glu, b_glu):
    b, t, _ = u.shape
    ug = u.astype(jnp.float32).reshape(b, t, N_SSM_GROUPS, SSM_GROUP)
    bu_re = jnp.einsum('btgc,gnc->btgn', ug, bb_re)
    bu_im = jnp.einsum('btgc,gnc->btgn', ug, bb_im)
    h0r = h0_re.astype(jnp.float32)
    h0i = h0_im.astype(jnp.float32)
    bu_re = bu_re.at[:, 0].add(a_re * h0r - a_im * h0i)
    bu_im = bu_im.at[:, 0].add(a_re * h0i + a_im * h0r)
    shape = (1, t, N_SSM_GROUPS, SSM_STATE)
    _, _, h_re, h_im = lax.associative_scan(
        _complex_linear_combine,
        (jnp.broadcast_to(a_re, shape), jnp.broadcast_to(a_im, shape), bu_re, bu_im),
        axis=1)
    y = (jnp.einsum('btgn,gcn->btgc', h_re, c_re.astype(jnp.float32))
         - jnp.einsum('btgn,gcn->btgc', h_im, c_im.astype(jnp.float32))
         + d_skip.astype(jnp.float32).reshape(N_SSM_GROUPS, SSM_GROUP) * ug)
    y = jax.nn.gelu(y.reshape(b, t, SSM_WIDTH))
    y = y * jax.nn.sigmoid(y @ w_glu.astype(jnp.float32) + b_glu.astype(jnp.float32))
    return y.astype(u.dtype), h_re[:, -1], h_im[:, -1]


def stick_breaking_weights(z, sb_bias, mask):
    zb = z + sb_bias.astype(jnp.float32)[None, :, None, None]
    log_keep = jnp.where(mask, jax.nn.log_sigmoid(-zb), 0.0)
    log_after = lax.cumsum(log_keep, axis=3, reverse=True) - log_keep
    return jnp.where(mask, jnp.exp(jax.nn.log_sigmoid(zb) + log_after), 0.0)


def prompt_stick_breaking(q, k, v, sb_bias):
    b, t, h, d = q.shape
    n_blk = t // Q_BLOCK
    qb = q.reshape(b, n_blk, Q_BLOCK, h, d).transpose(1, 0, 2, 3, 4)
    q_pos = jnp.arange(t, dtype=jnp.int32).reshape(n_blk, Q_BLOCK)
    k_pos = jnp.arange(t, dtype=jnp.int32)
    kf = k.astype(jnp.float32)
    vf = v.astype(jnp.float32)

    def block(args):
        q_blk, pos = args
        z = jnp.einsum('bqhd,bkhd->bhqk', q_blk.astype(jnp.float32), kf) * ATTN_SCALE
        w = stick_breaking_weights(z, sb_bias, k_pos[None, :] < pos[:, None])
        return jnp.einsum('bhqk,bkhd->bqhd', w, vf)

    out = lax.map(block, (qb, q_pos))
    return out.transpose(1, 0, 2, 3, 4).reshape(b, t, h * d).astype(q.dtype)


def gather_pages(pool, layer, page_table):
    g = pool[layer][page_table]
    nb, n_pages, page, h, d = g.shape
    return g.reshape(nb, n_pages * page, h, d).astype(jnp.float32)


def sample_stick_breaking(q, k_new, v_new, cache_k, cache_v, layer, page_table, sb_bias):
    b, t, h, d = q.shape
    past_len = page_table.shape[1] * cache_k.shape[2]
    qf = q.astype(jnp.float32)
    z = jnp.concatenate(
        [jnp.einsum('bqhd,bkhd->bhqk', qf, gather_pages(cache_k, layer, page_table)),
         jnp.einsum('bqhd,bkhd->bhqk', qf, k_new.astype(jnp.float32))], axis=3) * ATTN_SCALE
    q_pos = past_len + jnp.arange(t, dtype=jnp.int32)
    k_pos = jnp.arange(past_len + t, dtype=jnp.int32)
    w = stick_breaking_weights(z, sb_bias, k_pos[None, :] < q_pos[:, None])
    out = (jnp.einsum('bhqk,bkhd->bqhd', w[..., :past_len], gather_pages(cache_v, layer, page_table))
           + jnp.einsum('bhqk,bkhd->bqhd', w[..., past_len:], v_new.astype(jnp.float32)))
    return out.reshape(b, t, h * d).astype(q.dtype)


def setup_inputs(seed: int = 0) -> dict:
    key = jax.random.key(seed)
    ks = jax.random.split(key, 24)
    f32 = jnp.float32
    n_pages = PAST_LEN // PAGE_SIZE
    n_used = DEC_BATCH * n_pages
    n_pool = (n_used * POOL_NUM) // POOL_DEN
    nrm = lambda k, shape, s: s * jax.random.normal(k, shape, f32)
    lam_shape = (DEPTH, N_SSM_GROUPS, SSM_STATE)
    x_prompt = nrm(ks[0], (BATCH, SEQ, D_MODEL), 1.0)
    x_sample = nrm(ks[1], (DEC_BATCH, DEC_SEQ, D_MODEL), 1.0)
    cache_k = nrm(ks[2], (DEPTH, n_pool, PAGE_SIZE, N_HEADS, HEAD_DIM), 1.0)
    cache_v = nrm(ks[3], (DEPTH, n_pool, PAGE_SIZE, N_HEADS, HEAD_DIM), 1.0)
    state_ssm_re = nrm(ks[4], (DEPTH, DEC_BATCH, N_SSM_GROUPS, SSM_STATE), STATE_SCALE)
    state_ssm_im = nrm(ks[5], (DEPTH, DEC_BATCH, N_SSM_GROUPS, SSM_STATE), STATE_SCALE)
    page_table = jax.random.permutation(ks[6], n_pool)[:n_used].reshape(DEC_BATCH, n_pages).astype(jnp.int32)
    norm_gain = 1.0 + nrm(ks[7], (DEPTH, D_MODEL), 0.02)
    w_in = nrm(ks[8], (DEPTH, D_MODEL, IN_COLS), D_MODEL ** -0.5)
    sb_bias = SB_BIAS_INIT + nrm(ks[21], (DEPTH, N_HEADS), SB_BIAS_NOISE)
    lambda_re = -0.5 + nrm(ks[9], lam_shape, 0.01)
    lambda_im = math.pi * jnp.arange(SSM_STATE, dtype=f32) + nrm(ks[10], lam_shape, 0.01)
    log_dt = jax.random.uniform(ks[11], (DEPTH, N_SSM_GROUPS), f32, math.log(DT_MIN), math.log(DT_MAX))
    b_re = nrm(ks[12], (DEPTH, N_SSM_GROUPS, SSM_STATE, SSM_GROUP), (2 * SSM_GROUP) ** -0.5)
    b_im = nrm(ks[13], (DEPTH, N_SSM_GROUPS, SSM_STATE, SSM_GROUP), (2 * SSM_GROUP) ** -0.5)
    c_re = nrm(ks[14], (DEPTH, N_SSM_GROUPS, SSM_GROUP, SSM_STATE), SSM_STATE ** -0.5)
    c_im = nrm(ks[15], (DEPTH, N_SSM_GROUPS, SSM_GROUP, SSM_STATE), SSM_STATE ** -0.5)
    d_skip = 1.0 + nrm(ks[16], (DEPTH, SSM_WIDTH), 0.1)
    w_glu = nrm(ks[17], (DEPTH, SSM_WIDTH, SSM_WIDTH), SSM_WIDTH ** -0.5)
    b_glu = nrm(ks[18], (DEPTH, SSM_WIDTH), 0.01)
    w_out = nrm(ks[19], (DEPTH, MIX_WIDTH, D_MODEL), MIX_WIDTH ** -0.5)
    final_norm_gain = 1.0 + nrm(ks[20], (D_MODEL,), 0.02)
    return {'x_prompt': x_prompt, 'x_sample': x_sample,
            'cache_k': cache_k, 'cache_v': cache_v,
            'state_ssm_re': state_ssm_re, 'state_ssm_im': state_ssm_im,
            'page_table': page_table,
            'norm_gain': norm_gain, 'w_in': w_in, 'sb_bias': sb_bias,
            'lambda_re': lambda_re, 'lambda_im': lambda_im, 'log_dt': log_dt,
            'b_re': b_re, 'b_im': b_im, 'c_re': c_re, 'c_im': c_im, 'd_skip': d_skip,
            'w_glu': w_glu, 'b_glu': b_glu, 'w_out': w_out,
            'final_norm_gain': final_norm_gain}


def reference(x_prompt, x_sample, cache_k, cache_v, state_ssm_re, state_ssm_im, page_table,
              norm_gain, w_in, sb_bias, lambda_re, lambda_im, log_dt, b_re, b_im, c_re, c_im,
              d_skip, w_glu, b_glu, w_out, final_norm_gain):
    hp = x_prompt
    hs = x_sample
    k_p, v_p, sr_p, si_p = [], [], [], []
    k_s, v_s, sr_s, si_s = [], [], [], []
    for layer in range(DEPTH):
        disc = zoh_discretize(lambda_re[layer], lambda_im[layer], log_dt[layer], b_re[layer], b_im[layer])
        ssm_w = (c_re[layer], c_im[layer], d_skip[layer], w_glu[layer], b_glu[layer])

        u, zs, q, k, v, za = in_projection(hp, norm_gain[layer], w_in[layer])
        h0 = jnp.zeros((hp.shape[0], N_SSM_GROUPS, SSM_STATE), jnp.float32)
        ys, hr, hi = s5_branch(u, h0, h0, *disc, *ssm_w)
        ya = prompt_stick_breaking(q, k, v, sb_bias[layer])
        hp = hp + out_projection(ys, zs, ya, za, w_out[layer])
        k_p.append(k)
        v_p.append(v)
        sr_p.append(hr.astype(x_prompt.dtype))
        si_p.append(hi.astype(x_prompt.dtype))

        u, zs, q, k, v, za = in_projection(hs, norm_gain[layer], w_in[layer])
        ys, hr, hi = s5_branch(u, state_ssm_re[layer], state_ssm_im[layer], *disc, *ssm_w)
        ya = sample_stick_breaking(q, k, v, cache_k, cache_v, layer, page_table, sb_bias[layer])
        hs = hs + out_projection(ys, zs, ya, za, w_out[layer])
        k_s.append(k)
        v_s.append(v)
        sr_s.append(hr.astype(state_ssm_re.dtype))
        si_s.append(hi.astype(state_ssm_im.dtype))

    y_prompt = rmsnorm(hp, final_norm_gain)
    y_sample = rmsnorm(hs, final_norm_gain)
    return (y_prompt, y_sample,
            jnp.stack(k_p), jnp.stack(v_p), jnp.stack(sr_p), jnp.stack(si_p),
            jnp.stack(k_s), jnp.stack(v_s), jnp.stack(sr_s), jnp.stack(si_s))
```

```python
import functools
import math

import jax
import jax.numpy as jnp
from jax import lax
from jax.experimental import pallas as pl
from jax.experimental.pallas import tpu as pltpu

F32 = jnp.float32
BF16 = jnp.bfloat16

N_HEADS = 8
HEAD_DIM = 64
ATTN_WIDTH = N_HEADS * HEAD_DIM
SSM_WIDTH = 512
SSM_GROUP = 16
N_SSM_GROUPS = SSM_WIDTH // SSM_GROUP
SSM_STATE = 64
N_STATE = N_SSM_GROUPS * SSM_STATE
HALF_STATE = N_STATE // 2
HALF_CH = SSM_WIDTH // 2
ATTN_SCALE = HEAD_DIM ** -0.5
RMS_EPS = 1e-6

LANES = 128
SUBLANES = 8
SLABS_PER_HALF = 2 * HALF_STATE // LANES
VMEM_LIMIT = 56 * 1024 * 1024

PROMPT_NB = 8
PROMPT_LT = 128
ATTN_TQ = 256
PAGES_PER_STEP = 8
OUT_ROWS = 1024


def _const_spec(shape):
    nd = len(shape)
    return pl.BlockSpec(shape, lambda *_: (0,) * nd, pipeline_mode=pl.Buffered(1))


def _softplus(z):
    return jnp.maximum(z, 0.0) + jnp.log(1.0 + jnp.exp(-jnp.abs(z)))


def _suffix_sums(l, u):
    n = l.shape[0]
    l_hi = l.astype(BF16)
    l_lo = (l - l_hi.astype(F32)).astype(BF16)
    r = jnp.dot(jnp.concatenate([l_hi, l_lo], axis=0), u, preferred_element_type=F32)
    return r[:n] + r[n:]


def _discretize_kernel(lre_ref, lim_ref, ldt_ref, bre_ref, bim_ref,
                       are_ref, aim_ref, bbre_ref, bbim_ref):
    lr = lre_ref[...]
    li = lim_ref[...]
    dt = jnp.exp(ldt_ref[...])
    mag = jnp.exp(lr * dt)
    a_re = mag * jnp.cos(li * dt)
    a_im = mag * jnp.sin(li * dt)
    den = lr * lr + li * li
    g_re = ((a_re - 1.0) * lr + a_im * li) / den
    g_im = (a_im * lr - (a_re - 1.0) * li) / den
    br = bre_ref[...]
    bi = bim_ref[...]
    are_ref[...] = a_re
    aim_ref[...] = a_im
    bbre_ref[...] = g_re * br - g_im * bi
    bbim_ref[...] = g_re * bi + g_im * br


def _discretize(lam_re, lam_im, log_dt, b_re, b_im):
    col = lambda a: a.astype(F32).reshape(N_STATE, 1)
    ldt = jnp.broadcast_to(log_dt.astype(F32)[:, None], (N_SSM_GROUPS, SSM_STATE))
    outs = pl.pallas_call(
        _discretize_kernel,
        out_shape=[jax.ShapeDtypeStruct((N_STATE, 1), F32)] * 2
        + [jax.ShapeDtypeStruct((N_STATE, SSM_GROUP), F32)] * 2,
        name="discretize",
    )(col(lam_re), col(lam_im), col(ldt),
      b_re.astype(F32).reshape(N_STATE, SSM_GROUP), b_im.astype(F32).reshape(N_STATE, SSM_GROUP))
    a_re, a_im, bb_re, bb_im = outs
    shape3 = (N_SSM_GROUPS, SSM_STATE, SSM_GROUP)
    return a_re.reshape(1, N_STATE), a_im.reshape(1, N_STATE), bb_re.reshape(shape3), bb_im.reshape(shape3)


def _block_diag_weights(bb_re, bb_im, c_re, c_im):
    gh = N_SSM_GROUPS // 2
    eye = jnp.eye(gh, dtype=F32)
    bts, cts = [], []
    for hf in range(2):
        sl = slice(hf * gh, (hf + 1) * gh)
        bd_in = lambda b: jnp.einsum('gnc,gh->gchn', b[sl], eye).reshape(HALF_CH, HALF_STATE)
        bts.append(jnp.concatenate([bd_in(bb_re), bd_in(bb_im)], axis=1))
        bd_out = lambda c: jnp.einsum('gcn,gh->hngc', c[sl], eye).reshape(HALF_STATE, HALF_CH)
        cts.append(jnp.concatenate([bd_out(c_re.astype(F32)), -bd_out(c_im.astype(F32))], axis=0))
    return jnp.stack(bts).astype(BF16), jnp.stack(cts).astype(BF16)


def _trunk_in_kernel(x_ref, gain_ref, wa_ref, wkt_ref, wvt_ref, bt_ref, ct_ref, are_ref, aim_ref,
                     dskip_ref, wglu_ref, bglu_ref, h0re_ref, h0im_ref,
                     kt_ref, vt_ref, q_ref, za_ref, ms_ref, hre_ref, him_ref,
                     s_ref, *, nb, lt, pitch):
    r = nb * lt
    d_model = x_ref.shape[-1]
    tc = pl.program_id(1)

    @pl.when(tc == 0)
    def _():
        hre_ref[...] = h0re_ref[...]
        him_ref[...] = h0im_ref[...]

    x = x_ref[...].reshape(r, d_model)
    inv = lax.rsqrt(jnp.mean(x * x, axis=-1, keepdims=True) + RMS_EPS)
    xn = (x * inv * gain_ref[...]).astype(BF16)

    def proj(c):
        return jnp.dot(xn, wa_ref[:, c * 512:(c + 1) * 512], preferred_element_type=F32)

    nt_dims = (((1,), (1,)), ((), ()))
    ka, _, kl = kt_ref.shape
    kt = lax.dot_general(wkt_ref[...], xn, nt_dims, preferred_element_type=F32)
    for a in range(ka):
        kt_ref[a] = kt[:, a * kl:(a + 1) * kl]
    vt = lax.dot_general(wvt_ref[...], xn, nt_dims, preferred_element_type=F32)
    for a in range(ka):
        vt_ref[a] = vt[:, a * kl:(a + 1) * kl]

    q_ref[...] = (proj(2) * ATTN_SCALE).astype(BF16).reshape(q_ref.shape)
    za_ref[...] = proj(3).reshape(za_ref.shape)
    u = proj(0)
    u_bf = u.astype(BF16)

    def slab_rows(slab):
        if pitch == lt:
            return s_ref[slab, 0:r, :]
        return jnp.concatenate([s_ref[slab, b * pitch:b * pitch + lt, :] for b in range(nb)], axis=0)

    def scan_group(hf, g):
        row0 = g * (SUBLANES * pitch)
        bsl = pl.ds(g * SUBLANES if isinstance(g, int) else pl.multiple_of(g * SUBLANES, SUBLANES), SUBLANES)
        for cc in range(2):
            lane0 = hf * HALF_STATE + cc * 512
            lanes = [slice(lane0 + i * LANES, lane0 + (i + 1) * LANES) for i in range(4)]
            ar = [jnp.broadcast_to(are_ref[:, ls], (SUBLANES, LANES)) for ls in lanes]
            ai = [jnp.broadcast_to(aim_ref[:, ls], (SUBLANES, LANES)) for ls in lanes]
            hr0 = tuple(hre_ref[bsl, ls] for ls in lanes)
            hi0 = tuple(him_ref[bsl, ls] for ls in lanes)

            def step(t, carry):
                hr, hi = carry
                rows = pl.ds(row0 + t, SUBLANES, stride=pitch)
                nr, ni = [], []
                for i in range(4):
                    sre = cc * 4 + i
                    sim = SLABS_PER_HALF // 2 + cc * 4 + i
                    h_re = ar[i] * hr[i] - ai[i] * hi[i] + s_ref[sre, rows, :]
                    h_im = ar[i] * hi[i] + ai[i] * hr[i] + s_ref[sim, rows, :]
                    s_ref[sre, rows, :] = h_re
                    s_ref[sim, rows, :] = h_im
                    nr.append(h_re)
                    ni.append(h_im)
                return tuple(nr), tuple(ni)

            hr, hi = lax.fori_loop(0, lt, step, (hr0, hi0), unroll=min(lt, 4))
            for i, ls in enumerate(lanes):
                hre_ref[bsl, ls] = hr[i]
                him_ref[bsl, ls] = hi[i]

    y_halves = []
    for hf in range(2):
        u_half = u_bf[:, hf * HALF_CH:(hf + 1) * HALF_CH]
        for c in range(4):
            bu = jnp.dot(u_half, bt_ref[hf, :, c * 512:(c + 1) * 512], preferred_element_type=F32)
            for i in range(4):
                piece = bu[:, i * LANES:(i + 1) * LANES]
                if pitch == lt:
                    s_ref[c * 4 + i, 0:r, :] = piece
                else:
                    for b in range(nb):
                        s_ref[c * 4 + i, b * pitch:b * pitch + lt, :] = piece[b * lt:(b + 1) * lt]
        if nb == SUBLANES:
            scan_group(hf, 0)
        else:
            def group_body(g, carry, hf=hf):
                scan_group(hf, g)
                return carry
            lax.fori_loop(0, nb // SUBLANES, group_body, 0)
        y_h = jnp.zeros((r, HALF_CH), F32)
        for kc in range(4):
            hcat = jnp.concatenate([slab_rows(kc * 4 + i) for i in range(4)], axis=1).astype(BF16)
            y_h = y_h + jnp.dot(hcat, ct_ref[hf, kc * 512:(kc + 1) * 512, :], preferred_element_type=F32)
        y_halves.append(y_h)

    y = jnp.concatenate(y_halves, axis=1) + dskip_ref[...] * u
    y = jax.nn.gelu(y)
    gate = jax.nn.sigmoid(jnp.dot(y.astype(BF16), wglu_ref[...], preferred_element_type=F32) + bglu_ref[...])
    zs = proj(1)
    ms_ref[...] = (y * gate * (zs * jax.nn.sigmoid(zs))).astype(BF16).reshape(ms_ref.shape)


def _trunk_in(x3, weights, h0_re, h0_im, *, nb, lt, pitch, x_block, kt_block, kt_shape):
    (gain, wa, wkt, wvt, bt, ct, a_re, a_im, dskip, wglu, bglu) = weights
    xa, xr, d_model = x3.shape
    n_seq = h0_re.shape[0]
    grid = (n_seq // nb, (xa * xr) // (n_seq * lt))
    n_tc = grid[1]
    row_spec = lambda w: pl.BlockSpec((x_block[0], x_block[1], w), lambda g, t: (g, t, 0))
    state_spec = pl.BlockSpec((nb, N_STATE), lambda g, t: (g, 0))
    kt_spec = pl.BlockSpec(kt_block, lambda g, t: (g, 0, t))
    del n_tc
    out_shape = [
        jax.ShapeDtypeStruct(kt_shape, F32), jax.ShapeDtypeStruct(kt_shape, F32),
        jax.ShapeDtypeStruct((xa, xr, ATTN_WIDTH), BF16),
        jax.ShapeDtypeStruct((xa, xr, ATTN_WIDTH), F32),
        jax.ShapeDtypeStruct((xa, xr, SSM_WIDTH), BF16),
        jax.ShapeDtypeStruct((n_seq, N_STATE), F32), jax.ShapeDtypeStruct((n_seq, N_STATE), F32),
    ]
    return pl.pallas_call(
        functools.partial(_trunk_in_kernel, nb=nb, lt=lt, pitch=pitch),
        grid=grid,
        in_specs=[row_spec(d_model), _const_spec(gain.shape), _const_spec(wa.shape),
                  _const_spec(wkt.shape), _const_spec(wvt.shape), _const_spec(bt.shape),
                  _const_spec(ct.shape), _const_spec(a_re.shape), _const_spec(a_im.shape),
                  _const_spec(dskip.shape), _const_spec(wglu.shape), _const_spec(bglu.shape),
                  state_spec, state_spec],
        out_specs=[kt_spec, kt_spec, row_spec(ATTN_WIDTH), row_spec(ATTN_WIDTH), row_spec(SSM_WIDTH),
                   state_spec, state_spec],
        out_shape=out_shape,
        scratch_shapes=[pltpu.VMEM((SLABS_PER_HALF, nb * pitch, LANES), F32)],
        compiler_params=pltpu.CompilerParams(
            dimension_semantics=("parallel", "arbitrary"), vmem_limit_bytes=VMEM_LIMIT),
        name="trunk_in",
    )(x3, gain, wa, wkt, wvt, bt, ct, a_re, a_im, dskip, wglu, bglu, h0_re, h0_im)


def _prompt_attn_kernel(bias_ref, q_ref, kt_ref, vt_ref, za_ref, u_ref, o_ref, ktb_ref, vtb_ref, *, tq):
    qi = pl.program_id(1)

    @pl.when(qi == 0)
    def _():
        for j in range(ktb_ref.shape[0]):
            ktb_ref[j] = kt_ref[0, :, j * tq:(j + 1) * tq].astype(BF16)
            vtb_ref[j] = vt_ref[0, :, j * tq:(j + 1) * tq].astype(BF16)

    row = lax.broadcasted_iota(jnp.int32, (tq, tq), 0)
    col = lax.broadcasted_iota(jnp.int32, (tq, tq), 1)
    visible = col < row
    u = u_ref[...]
    nt_dims = (((1,), (1,)), ((), ()))
    outs = []
    for h in range(N_HEADS):
        hd = slice(h * HEAD_DIM, (h + 1) * HEAD_DIM)
        qh = q_ref[0, :, hd]
        bias = bias_ref[h]

        def block(kb, acc, carry, diagonal):
            z = jnp.dot(qh, ktb_ref[kb, hd, :], preferred_element_type=F32) + bias
            l = _softplus(z)
            if diagonal:
                l = jnp.where(visible, l, 0.0)
            logw = z - l - _suffix_sums(l, u) - carry
            w = jnp.exp(logw)
            if diagonal:
                w = jnp.where(visible, w, 0.0)
            acc = acc + lax.dot_general(w.astype(BF16), vtb_ref[kb, hd, :], nt_dims,
                                        preferred_element_type=F32)
            return acc, carry + jnp.sum(l, axis=1, keepdims=True)

        acc, carry = block(qi, jnp.zeros((tq, HEAD_DIM), F32), jnp.zeros((tq, 1), F32), True)
        acc, carry = lax.fori_loop(0, qi, lambda j, c: block(qi - 1 - j, c[0], c[1], False), (acc, carry))
        outs.append(acc)
    za = za_ref[0]
    o_ref[0] = (jnp.concatenate(outs, axis=1) * (za * jax.nn.sigmoid(za))).astype(BF16)


def _prompt_attn(sb_bias, q, kt, vt, za, u):
    n_b, t, _ = q.shape
    tq = ATTN_TQ
    q_spec = pl.BlockSpec((1, tq, ATTN_WIDTH), lambda b, i: (b, i, 0))
    kv_spec = pl.BlockSpec((1, ATTN_WIDTH, t), lambda b, i: (b, 0, 0))
    return pl.pallas_call(
        functools.partial(_prompt_attn_kernel, tq=tq),
        grid=(n_b, t // tq),
        in_specs=[pl.BlockSpec(memory_space=pltpu.SMEM), q_spec, kv_spec, kv_spec, q_spec,
                  _const_spec(u.shape)],
        out_specs=q_spec,
        out_shape=jax.ShapeDtypeStruct((n_b, t, ATTN_WIDTH), BF16),
        scratch_shapes=[pltpu.VMEM((t // tq, ATTN_WIDTH, tq), BF16)] * 2,
        compiler_params=pltpu.CompilerParams(
            dimension_semantics=("parallel", "arbitrary"), vmem_limit_bytes=VMEM_LIMIT),
        name="prompt_attn",
    )(sb_bias, q, kt, vt, za, u)


def _sample_attn_kernel(pt_ref, q_ref, kn_ref, vn_ref, za_ref, bias_ref, u_ref, *rest, n_pages):
    del pt_ref
    k_pages = rest[:n_pages]
    v_pages = rest[n_pages:2 * n_pages]
    o_ref, acc_ref, carry_ref, qbd_ref = rest[2 * n_pages:]
    c = pl.program_id(1)
    n_q = q_ref.shape[1]
    rows = n_q * N_HEADS
    nt_dims = (((1,), (1,)), ((), ()))
    bias = bias_ref[...]

    @pl.when(c == 0)
    def _():
        q = q_ref[0].astype(F32)
        head_of_lane = lax.broadcasted_iota(jnp.int32, (N_HEADS, ATTN_WIDTH), 1) // HEAD_DIM
        own = head_of_lane == lax.broadcasted_iota(jnp.int32, (N_HEADS, ATTN_WIDTH), 0)
        qbd = jnp.concatenate(
            [jnp.where(own, jnp.broadcast_to(q[i:i + 1], (N_HEADS, ATTN_WIDTH)), 0.0) for i in range(n_q)],
            axis=0).astype(BF16)
        qbd_ref[...] = qbd
        pad = jnp.zeros((LANES - kn_ref.shape[1], ATTN_WIDTH), F32)
        kn = jnp.concatenate([kn_ref[0], pad], axis=0).astype(BF16)
        vn = jnp.concatenate([vn_ref[0], pad], axis=0).astype(BF16)
        z = lax.dot_general(qbd, kn, nt_dims, preferred_element_type=F32) + bias
        q_idx = lax.broadcasted_iota(jnp.int32, (rows, LANES), 0) // N_HEADS
        visible = lax.broadcasted_iota(jnp.int32, (rows, LANES), 1) < q_idx
        l = jnp.where(visible, _softplus(z), 0.0)
        logw = z - l - _suffix_sums(l, u_ref[0:LANES, 0:LANES])
        w = jnp.where(visible, jnp.exp(logw), 0.0)
        acc_ref[...] = jnp.dot(w.astype(BF16), vn, preferred_element_type=F32)
        carry_ref[...] = jnp.sum(l, axis=1, keepdims=True)

    qbd = qbd_ref[...]
    u = u_ref[...]
    acc = acc_ref[...]
    carry = carry_ref[...]
    for p in reversed(range(n_pages // 2)):
        page_t = lambda ref: ref[...].reshape(ATTN_WIDTH, LANES)
        kt = jnp.concatenate([page_t(k_pages[2 * p]), page_t(k_pages[2 * p + 1])], axis=1).astype(BF16)
        vt = jnp.concatenate([page_t(v_pages[2 * p]), page_t(v_pages[2 * p + 1])], axis=1).astype(BF16)
        z = jnp.dot(qbd, kt, preferred_element_type=F32) + bias
        l = _softplus(z)
        w = jnp.exp(z - l - _suffix_sums(l, u) - carry)
        acc = acc + lax.dot_general(w.astype(BF16), vt, nt_dims, preferred_element_type=F32)
        carry = carry + jnp.sum(l, axis=1, keepdims=True)
    acc_ref[...] = acc
    carry_ref[...] = carry

    @pl.when(c == pl.num_programs(1) - 1)
    def _():
        head_of_lane = lax.broadcasted_iota(jnp.int32, (rows, ATTN_WIDTH), 1) // HEAD_DIM
        own = head_of_lane == lax.broadcasted_iota(jnp.int32, (rows, ATTN_WIDTH), 0) % N_HEADS
        picked = jnp.where(own, acc, 0.0).reshape(n_q, N_HEADS, ATTN_WIDTH).sum(axis=1)
        za = za_ref[0]
        o_ref[0] = (picked * (za * jax.nn.sigmoid(za))).astype(BF16)


def _sample_attn(page_table, q, kn, vn, za, bias_rows, u, pool_kt, pool_vt):
    n_b, n_q, _ = q.shape
    n_logical = page_table.shape[1]
    npg = PAGES_PER_STEP
    n_chunks = n_logical // npg
    pt_flat = page_table.reshape(-1)
    per_b = lambda w: pl.BlockSpec((1, w, ATTN_WIDTH), lambda b, c, pt: (b, 0, 0))

    def page_spec(i):
        def index(b, c, pt):
            return (pt[b * n_logical + n_logical - npg * (c + 1) + i], 0, 0, 0)
        return pl.BlockSpec((None, N_HEADS, HEAD_DIM, LANES), index)

    rows = n_q * N_HEADS
    const = lambda shape: pl.BlockSpec(shape, lambda b, c, pt: (0,) * len(shape))
    grid_spec = pltpu.PrefetchScalarGridSpec(
        num_scalar_prefetch=1,
        grid=(n_b, n_chunks),
        in_specs=[per_b(n_q), per_b(kn.shape[1]), per_b(vn.shape[1]), per_b(n_q),
                  const(bias_rows.shape), const(u.shape)]
        + [page_spec(i) for i in range(npg)] * 2,
        out_specs=per_b(n_q),
        scratch_shapes=[pltpu.VMEM((rows, ATTN_WIDTH), F32), pltpu.VMEM((rows, 1), F32),
                        pltpu.VMEM((rows, ATTN_WIDTH), BF16)],
    )
    return pl.pallas_call(
        functools.partial(_sample_attn_kernel, n_pages=npg),
        grid_spec=grid_spec,
        out_shape=jax.ShapeDtypeStruct((n_b, n_q, ATTN_WIDTH), BF16),
        compiler_params=pltpu.CompilerParams(
            dimension_semantics=("parallel", "arbitrary"), vmem_limit_bytes=VMEM_LIMIT),
        name="sample_attn",
    )(pt_flat, q, kn, vn, za, bias_rows, u, *([pool_kt] * npg), *([pool_vt] * npg))


def _out_proj_kernel(x_ref, ms_ref, ma_ref, wo_ref, gain_ref, y_ref):
    h = (x_ref[...]
         + jnp.dot(ms_ref[...], wo_ref[0:SSM_WIDTH, :], preferred_element_type=F32)
         + jnp.dot(ma_ref[...], wo_ref[SSM_WIDTH:, :], preferred_element_type=F32))
    inv = lax.rsqrt(jnp.mean(h * h, axis=-1, keepdims=True) + RMS_EPS)
    y_ref[...] = h * inv * gain_ref[...]


def _out_proj(x2, ms, ma, wo, gain):
    n_rows, d_model = x2.shape
    tr = min(OUT_ROWS, n_rows)
    rows = lambda w: pl.BlockSpec((tr, w), lambda i: (i, 0))
    return pl.pallas_call(
        _out_proj_kernel,
        grid=(n_rows // tr,),
        in_specs=[rows(d_model), rows(SSM_WIDTH), rows(ATTN_WIDTH), _const_spec(wo.shape),
                  _const_spec(gain.shape)],
        out_specs=rows(d_model),
        out_shape=jax.ShapeDtypeStruct((n_rows, d_model), F32),
        compiler_params=pltpu.CompilerParams(
            dimension_semantics=("parallel",), vmem_limit_bytes=VMEM_LIMIT),
        name="out_proj",
    )(x2, ms, ma, wo, gain)


def kernel(x_prompt, x_sample, cache_k, cache_v, state_ssm_re, state_ssm_im, page_table, norm_gain, w_in, sb_bias, lambda_re, lambda_im, log_dt, b_re, b_im, c_re, c_im, d_skip, w_glu, b_glu, w_out, final_norm_gain):
    assert w_in.shape[0] == 1, "single-layer trunk only"
    n_b, t, d_model = x_prompt.shape
    n_s, n_q, _ = x_sample.shape
    assert (d_model, w_in.shape[2]) == (1024, 2 * SSM_WIDTH + 4 * ATTN_WIDTH)
    assert n_b % PROMPT_NB == 0 and t % PROMPT_LT == 0 and t % ATTN_TQ == 0 and n_s % SUBLANES == 0

    a_re, a_im, bb_re, bb_im = _discretize(lambda_re[0], lambda_im[0], log_dt[0], b_re[0], b_im[0])
    bt, ct = _block_diag_weights(bb_re, bb_im, c_re[0], c_im[0])
    w = w_in[0].astype(BF16)
    cut = 2 * SSM_WIDTH
    wa = jnp.concatenate([w[:, :cut + ATTN_WIDTH], w[:, cut + 3 * ATTN_WIDTH:]], axis=1)
    wkt = w[:, cut + ATTN_WIDTH:cut + 2 * ATTN_WIDTH].T
    wvt = w[:, cut + 2 * ATTN_WIDTH:cut + 3 * ATTN_WIDTH].T
    row = lambda a: a.astype(F32).reshape(1, -1)
    weights = (row(norm_gain[0]), wa, wkt, wvt, bt, ct, a_re, a_im, row(d_skip[0]),
               w_glu[0].astype(BF16), row(b_glu[0]))
    wo = w_out[0].astype(BF16)
    gain_f = row(final_norm_gain)
    tri = lax.broadcasted_iota(jnp.int32, (ATTN_TQ, ATTN_TQ), 0) > lax.broadcasted_iota(jnp.int32, (ATTN_TQ, ATTN_TQ), 1)
    u = tri.astype(BF16)

    zeros = jnp.zeros((n_b, N_STATE), F32)
    pitch = PROMPT_LT + SUBLANES
    kt_p, vt_p, q_p, za_p, ms_p, hre_p, him_p = _trunk_in(
        x_prompt, weights, zeros, zeros, nb=PROMPT_NB, lt=PROMPT_LT, pitch=pitch,
        x_block=(PROMPT_NB, PROMPT_LT), kt_block=(PROMPT_NB, ATTN_WIDTH, PROMPT_LT),
        kt_shape=(n_b, ATTN_WIDTH, t))
    ma_p = _prompt_attn(sb_bias[0].astype(F32), q_p, kt_p, vt_p, za_p, u)
    y_prompt = _out_proj(x_prompt.reshape(n_b * t, d_model), ms_p.reshape(n_b * t, SSM_WIDTH),
                         ma_p.reshape(n_b * t, ATTN_WIDTH), wo, gain_f).reshape(n_b, t, d_model)
    heads_out = lambda a: a.reshape(1, n_b, N_HEADS, HEAD_DIM, t).transpose(0, 1, 4, 2, 3)
    state_out = lambda a, n: a.reshape(1, n, N_SSM_GROUPS, SSM_STATE)

    rows_s = n_s * n_q
    kt_s, vt_s, q_s, za_s, ms_s, hre_s, him_s = _trunk_in(
        x_sample.reshape(1, rows_s, d_model), weights,
        state_ssm_re[0].reshape(n_s, N_STATE).astype(F32), state_ssm_im[0].reshape(n_s, N_STATE).astype(F32),
        nb=n_s, lt=n_q, pitch=n_q, x_block=(1, rows_s), kt_block=(1, ATTN_WIDTH, rows_s),
        kt_shape=(1, ATTN_WIDTH, rows_s))
    new_rows = lambda a: a[0].T.reshape(n_s, n_q, ATTN_WIDTH)
    k_new, v_new = new_rows(kt_s), new_rows(vt_s)
    pad8 = lambda a: jnp.pad(a, ((0, 0), (0, SUBLANES - n_q), (0, 0)))
    bias_rows = jnp.tile(sb_bias[0].astype(F32), n_q).reshape(n_q * N_HEADS, 1)
    pool_t = lambda c: jnp.transpose(c[0], (0, 2, 3, 1))
    ma_s = _sample_attn(page_table, q_s.reshape(n_s, n_q, ATTN_WIDTH), pad8(k_new), pad8(v_new),
                        za_s.reshape(n_s, n_q, ATTN_WIDTH), bias_rows, u, pool_t(cache_k), pool_t(cache_v))
    y_sample = _out_proj(x_sample.reshape(rows_s, d_model), ms_s.reshape(rows_s, SSM_WIDTH),
                         ma_s.reshape(rows_s, ATTN_WIDTH), wo, gain_f).reshape(n_s, n_q, d_model)
    new_out = lambda a: a.reshape(1, n_s, n_q, N_HEADS, HEAD_DIM)

    return (y_prompt, y_sample,
            heads_out(kt_p), heads_out(vt_p), state_out(hre_p, n_b), state_out(him_p, n_b),
            new_out(k_new), new_out(v_new), state_out(hre_s, n_s), state_out(him_s, n_s))
```

```python
import functools
import math

import jax
import jax.numpy as jnp
from jax import lax
from jax.experimental import pallas as pl
from jax.experimental.pallas import tpu as pltpu

F32 = jnp.float32
BF16 = jnp.bfloat16

N_HEADS = 8
HEAD_DIM = 64
ATTN_WIDTH = N_HEADS * HEAD_DIM
SSM_WIDTH = 512
SSM_GROUP = 16
N_SSM_GROUPS = SSM_WIDTH // SSM_GROUP
SSM_STATE = 64
N_STATE = N_SSM_GROUPS * SSM_STATE
HALF_STATE = N_STATE // 2
HALF_CH = SSM_WIDTH // 2
ATTN_SCALE = HEAD_DIM ** -0.5
RMS_EPS = 1e-6
LOG2E = math.log2(math.e)

LANES = 128
SUBLANES = 8
SLABS_PER_HALF = 2 * HALF_STATE // LANES
VMEM_LIMIT = 56 * 1024 * 1024

PROMPT_NB = 8
PROMPT_LT = 128
ATTN_TQ = 256
PAGES_PER_STEP = 16
OUT_ROWS = 1024


def _const_spec(shape):
    nd = len(shape)
    return pl.BlockSpec(shape, lambda *_: (0,) * nd, pipeline_mode=pl.Buffered(1))


def _softplus(z):
    return jnp.maximum(z, 0.0) + jnp.log(1.0 + jnp.exp2(jnp.abs(z) * (-LOG2E)))


def _suffix_sums(l, u2):
    l_hi = l.astype(BF16)
    l_lo = (l - l_hi.astype(F32)).astype(BF16)
    return jnp.dot(jnp.concatenate([l_hi, l_lo], axis=1), u2, preferred_element_type=F32)


def _suffix_sums_bf16(l, u):
    return jnp.dot(l.astype(BF16), u, preferred_element_type=F32)


def _discretize_kernel(lre_ref, lim_ref, ldt_ref, bre_ref, bim_ref,
                       are_ref, aim_ref, bbre_ref, bbim_ref):
    lr = lre_ref[...]
    li = lim_ref[...]
    dt = jnp.exp(ldt_ref[...])
    mag = jnp.exp(lr * dt)
    a_re = mag * jnp.cos(li * dt)
    a_im = mag * jnp.sin(li * dt)
    den = lr * lr + li * li
    g_re = ((a_re - 1.0) * lr + a_im * li) / den
    g_im = (a_im * lr - (a_re - 1.0) * li) / den
    br = bre_ref[...]
    bi = bim_ref[...]
    are_ref[...] = a_re
    aim_ref[...] = a_im
    bbre_ref[...] = g_re * br - g_im * bi
    bbim_ref[...] = g_re * bi + g_im * br


def _discretize(lam_re, lam_im, log_dt, b_re, b_im):
    col = lambda a: a.astype(F32).reshape(N_STATE, 1)
    ldt = jnp.broadcast_to(log_dt.astype(F32)[:, None], (N_SSM_GROUPS, SSM_STATE))
    outs = pl.pallas_call(
        _discretize_kernel,
        out_shape=[jax.ShapeDtypeStruct((N_STATE, 1), F32)] * 2
        + [jax.ShapeDtypeStruct((N_STATE, SSM_GROUP), F32)] * 2,
        name="discretize",
    )(col(lam_re), col(lam_im), col(ldt),
      b_re.astype(F32).reshape(N_STATE, SSM_GROUP), b_im.astype(F32).reshape(N_STATE, SSM_GROUP))
    a_re, a_im, bb_re, bb_im = outs
    shape3 = (N_SSM_GROUPS, SSM_STATE, SSM_GROUP)
    return a_re.reshape(1, N_STATE), a_im.reshape(1, N_STATE), bb_re.reshape(shape3), bb_im.reshape(shape3)


def _block_diag_weights(bb_re, bb_im, c_re, c_im):
    gh = N_SSM_GROUPS // 2
    eye = jnp.eye(gh, dtype=F32)
    bts, cts = [], []
    for hf in range(2):
        sl = slice(hf * gh, (hf + 1) * gh)
        bd_in = lambda b: jnp.einsum('gnc,gh->gchn', b[sl], eye).reshape(HALF_CH, HALF_STATE)
        bts.append(jnp.concatenate([bd_in(bb_re), bd_in(bb_im)], axis=1))
        bd_out = lambda c: jnp.einsum('gcn,gh->hngc', c[sl], eye).reshape(HALF_STATE, HALF_CH)
        cts.append(jnp.concatenate([bd_out(c_re.astype(F32)), -bd_out(c_im.astype(F32))], axis=0))
    return jnp.stack(bts).astype(BF16), jnp.stack(cts).astype(BF16)


def _trunk_in_kernel(x_ref, gain_ref, wa_ref, wkt_ref, wvt_ref, bt_ref, ct_ref, are_ref, aim_ref,
                     dskip_ref, wglu_ref, bglu_ref, h0re_ref, h0im_ref,
                     kt_ref, vt_ref, q_ref, za_ref, ms_ref, hre_ref, him_ref,
                     s_ref, *, nb, lt, pitch):
    r = nb * lt
    d_model = x_ref.shape[-1]
    tc = pl.program_id(1)

    @pl.when(tc == 0)
    def _():
        hre_ref[...] = h0re_ref[...]
        him_ref[...] = h0im_ref[...]

    x = x_ref[...].reshape(r, d_model)
    inv = lax.rsqrt(jnp.mean(x * x, axis=-1, keepdims=True) + RMS_EPS)
    xn = (x * inv * gain_ref[...]).astype(BF16)

    def proj(c):
        return jnp.dot(xn, wa_ref[:, c * 512:(c + 1) * 512], preferred_element_type=F32)

    nt_dims = (((1,), (1,)), ((), ()))
    ka, _, kl = kt_ref.shape
    kt = lax.dot_general(wkt_ref[...], xn, nt_dims, preferred_element_type=F32)
    for a in range(ka):
        kt_ref[a] = kt[:, a * kl:(a + 1) * kl]
    vt = lax.dot_general(wvt_ref[...], xn, nt_dims, preferred_element_type=F32)
    for a in range(ka):
        vt_ref[a] = vt[:, a * kl:(a + 1) * kl]

    q_ref[...] = (proj(2) * ATTN_SCALE).astype(BF16).reshape(q_ref.shape)
    za_ref[...] = proj(3).reshape(za_ref.shape)
    u = proj(0)
    u_bf = u.astype(BF16)

    def slab_rows(slab):
        if pitch == lt:
            return s_ref[slab, 0:r, :]
        return jnp.concatenate([s_ref[slab, b * pitch:b * pitch + lt, :] for b in range(nb)], axis=0)

    def scan_group(hf, g):
        row0 = g * (SUBLANES * pitch)
        bsl = pl.ds(g * SUBLANES if isinstance(g, int) else pl.multiple_of(g * SUBLANES, SUBLANES), SUBLANES)
        for cc in range(2):
            lane0 = hf * HALF_STATE + cc * 512
            lanes = [slice(lane0 + i * LANES, lane0 + (i + 1) * LANES) for i in range(4)]
            ar = [jnp.broadcast_to(are_ref[:, ls], (SUBLANES, LANES)) for ls in lanes]
            ai = [jnp.broadcast_to(aim_ref[:, ls], (SUBLANES, LANES)) for ls in lanes]
            hr0 = tuple(hre_ref[bsl, ls] for ls in lanes)
            hi0 = tuple(him_ref[bsl, ls] for ls in lanes)

            def step(t, carry):
                hr, hi = carry
                rows = pl.ds(row0 + t, SUBLANES, stride=pitch)
                nr, ni = [], []
                for i in range(4):
                    sre = cc * 4 + i
                    sim = SLABS_PER_HALF // 2 + cc * 4 + i
                    h_re = ar[i] * hr[i] - ai[i] * hi[i] + s_ref[sre, rows, :]
                    h_im = ar[i] * hi[i] + ai[i] * hr[i] + s_ref[sim, rows, :]
                    s_ref[sre, rows, :] = h_re
                    s_ref[sim, rows, :] = h_im
                    nr.append(h_re)
                    ni.append(h_im)
                return tuple(nr), tuple(ni)

            hr, hi = lax.fori_loop(0, lt, step, (hr0, hi0), unroll=min(lt, 4))
            for i, ls in enumerate(lanes):
                hre_ref[bsl, ls] = hr[i]
                him_ref[bsl, ls] = hi[i]

    y_halves = []
    for hf in range(2):
        u_half = u_bf[:, hf * HALF_CH:(hf + 1) * HALF_CH]
        for c in range(4):
            bu = jnp.dot(u_half, bt_ref[hf, :, c * 512:(c + 1) * 512], preferred_element_type=F32)
            for i in range(4):
                piece = bu[:, i * LANES:(i + 1) * LANES]
                if pitch == lt:
                    s_ref[c * 4 + i, 0:r, :] = piece
                else:
                    for b in range(nb):
                        s_ref[c * 4 + i, b * pitch:b * pitch + lt, :] = piece[b * lt:(b + 1) * lt]
        if nb == SUBLANES:
            scan_group(hf, 0)
        else:
            def group_body(g, carry, hf=hf):
                scan_group(hf, g)
                return carry
            lax.fori_loop(0, nb // SUBLANES, group_body, 0)
        y_h = jnp.zeros((r, HALF_CH), F32)
        for kc in range(4):
            hcat = jnp.concatenate([slab_rows(kc * 4 + i) for i in range(4)], axis=1).astype(BF16)
            y_h = y_h + jnp.dot(hcat, ct_ref[hf, kc * 512:(kc + 1) * 512, :], preferred_element_type=F32)
        y_halves.append(y_h)

    y = jnp.concatenate(y_halves, axis=1) + dskip_ref[...] * u
    y = jax.nn.gelu(y)
    gate = jax.nn.sigmoid(jnp.dot(y.astype(BF16), wglu_ref[...], preferred_element_type=F32) + bglu_ref[...])
    zs = proj(1)
    ms_ref[...] = (y * gate * (zs * jax.nn.sigmoid(zs))).astype(BF16).reshape(ms_ref.shape)


def _trunk_in(x3, weights, h0_re, h0_im, *, nb, lt, pitch, x_block, kt_block, kt_shape):
    (gain, wa, wkt, wvt, bt, ct, a_re, a_im, dskip, wglu, bglu) = weights
    xa, xr, d_model = x3.shape
    n_seq = h0_re.shape[0]
    grid = (n_seq // nb, (xa * xr) // (n_seq * lt))
    n_tc = grid[1]
    row_spec = lambda w: pl.BlockSpec((x_block[0], x_block[1], w), lambda g, t: (g, t, 0))
    state_spec = pl.BlockSpec((nb, N_STATE), lambda g, t: (g, 0))
    kt_spec = pl.BlockSpec(kt_block, lambda g, t: (g, 0, t))
    del n_tc
    out_shape = [
        jax.ShapeDtypeStruct(kt_shape, F32), jax.ShapeDtypeStruct(kt_shape, F32),
        jax.ShapeDtypeStruct((xa, xr, ATTN_WIDTH), BF16),
        jax.ShapeDtypeStruct((xa, xr, ATTN_WIDTH), F32),
        jax.ShapeDtypeStruct((xa, xr, SSM_WIDTH), BF16),
        jax.ShapeDtypeStruct((n_seq, N_STATE), F32), jax.ShapeDtypeStruct((n_seq, N_STATE), F32),
    ]
    return pl.pallas_call(
        functools.partial(_trunk_in_kernel, nb=nb, lt=lt, pitch=pitch),
        grid=grid,
        in_specs=[row_spec(d_model), _const_spec(gain.shape), _const_spec(wa.shape),
                  _const_spec(wkt.shape), _const_spec(wvt.shape), _const_spec(bt.shape),
                  _const_spec(ct.shape), _const_spec(a_re.shape), _const_spec(a_im.shape),
                  _const_spec(dskip.shape), _const_spec(wglu.shape), _const_spec(bglu.shape),
                  state_spec, state_spec],
        out_specs=[kt_spec, kt_spec, row_spec(ATTN_WIDTH), row_spec(ATTN_WIDTH), row_spec(SSM_WIDTH),
                   state_spec, state_spec],
        out_shape=out_shape,
        scratch_shapes=[pltpu.VMEM((SLABS_PER_HALF, nb * pitch, LANES), F32)],
        compiler_params=pltpu.CompilerParams(
            dimension_semantics=("parallel", "arbitrary"), vmem_limit_bytes=VMEM_LIMIT),
        name="trunk_in",
    )(x3, gain, wa, wkt, wvt, bt, ct, a_re, a_im, dskip, wglu, bglu, h0_re, h0_im)


def _prompt_attn_kernel(bias_ref, q_ref, kt_ref, vt_ref, za_ref, u_ref, o_ref,
                        ktb_ref, vtb_ref, qh_ref, acc_ref, carry_ref, *, tq):
    qi = pl.program_id(1)
    n_blk = ktb_ref.shape[0]
    nt_dims = (((1,), (1,)), ((), ()))

    @pl.when(qi == 0)
    def _():
        aug_row = lax.broadcasted_iota(jnp.int32, (HEAD_DIM, tq), 0)
        zeros = jnp.zeros((HEAD_DIM, tq), BF16)
        for h in range(N_HEADS):
            hd = slice(h * HEAD_DIM, (h + 1) * HEAD_DIM)
            b = jnp.full((HEAD_DIM, tq), bias_ref[h], F32)
            b_hi = b.astype(BF16).astype(F32)
            b_mid = (b - b_hi).astype(BF16).astype(F32)
            b_lo = b - b_hi - b_mid
            aug = jnp.where(aug_row == 0, b_hi, jnp.where(aug_row == 1, b_mid,
                                                         jnp.where(aug_row == 2, b_lo, 0.0))).astype(BF16)
            for j in range(n_blk):
                keys = slice(j * tq, (j + 1) * tq)
                ktb_ref[j, h, 0:HEAD_DIM, :] = kt_ref[0, hd, keys].astype(BF16)
                ktb_ref[j, h, HEAD_DIM:, :] = aug
                vt = vt_ref[0, hd, keys].astype(BF16)
                vtb_ref[j, h, 0:HEAD_DIM, :] = vt if h % 2 == 0 else zeros
                vtb_ref[j, h, HEAD_DIM:, :] = zeros if h % 2 == 0 else vt

    ones = (lax.broadcasted_iota(jnp.int32, (tq, HEAD_DIM), 1) < 3).astype(F32).astype(BF16)
    for h in range(N_HEADS):
        qh_ref[h] = jnp.concatenate([q_ref[0, :, h * HEAD_DIM:(h + 1) * HEAD_DIM], ones], axis=1)

    row = lax.broadcasted_iota(jnp.int32, (tq, tq), 0)
    col = lax.broadcasted_iota(jnp.int32, (tq, tq), 1)
    visible = col < row
    u = u_ref[0:tq, :]

    def block(kb, diagonal):
        heads = range(N_HEADS)
        zs = [jnp.dot(qh_ref[h], ktb_ref[kb, h], preferred_element_type=F32) for h in heads]
        ls = [_softplus(z) for z in zs]
        if diagonal:
            ls = [jnp.where(visible, l, 0.0) for l in ls]
        css = [_suffix_sums_bf16(l, u) for l in ls]
        ws = []
        for h in heads:
            total = css[h][:, 0:1] + ls[h][:, 0:1]
            if diagonal:
                ws.append(jnp.where(visible, jnp.exp(zs[h] - ls[h] - css[h]), 0.0).astype(BF16))
                carry_ref[h] = total
            else:
                carry = carry_ref[h]
                ws.append(jnp.exp(zs[h] - ls[h] - css[h] - carry).astype(BF16))
                carry_ref[h] = carry + total
        pvs = [lax.dot_general(ws[h], vtb_ref[kb, h], nt_dims, preferred_element_type=F32) for h in heads]
        for p in range(N_HEADS // 2):
            if diagonal:
                acc_ref[p] = pvs[2 * p] + pvs[2 * p + 1]
            else:
                acc_ref[p] += pvs[2 * p] + pvs[2 * p + 1]

    block(qi, True)

    def body(j, c):
        block(qi - 1 - j, False)
        return c
    lax.fori_loop(0, qi, body, 0)

    za = za_ref[0]
    o = jnp.concatenate([acc_ref[p] for p in range(N_HEADS // 2)], axis=1)
    o_ref[0] = (o * (za * jax.nn.sigmoid(za))).astype(BF16)


def _prompt_attn(sb_bias, q, kt, vt, za, u):
    n_b, t, _ = q.shape
    tq = ATTN_TQ
    q_spec = pl.BlockSpec((1, tq, ATTN_WIDTH), lambda b, i: (b, i, 0))
    kv_spec = pl.BlockSpec((1, ATTN_WIDTH, t), lambda b, i: (b, 0, 0))
    return pl.pallas_call(
        functools.partial(_prompt_attn_kernel, tq=tq),
        grid=(n_b, t // tq),
        in_specs=[pl.BlockSpec(memory_space=pltpu.SMEM), q_spec, kv_spec, kv_spec, q_spec,
                  _const_spec(u.shape)],
        out_specs=q_spec,
        out_shape=jax.ShapeDtypeStruct((n_b, t, ATTN_WIDTH), BF16),
        scratch_shapes=[pltpu.VMEM((t // tq, N_HEADS, 2 * HEAD_DIM, tq), BF16)] * 2
        + [pltpu.VMEM((N_HEADS, tq, 2 * HEAD_DIM), BF16), pltpu.VMEM((N_HEADS // 2, tq, 2 * HEAD_DIM), F32),
           pltpu.VMEM((N_HEADS, tq, 1), F32)],
        compiler_params=pltpu.CompilerParams(
            dimension_semantics=("parallel", "arbitrary"), vmem_limit_bytes=VMEM_LIMIT),
        name="prompt_attn",
    )(sb_bias, q, kt, vt, za, u)


def _sample_attn_kernel(pt_ref, q_ref, kn_ref, vn_ref, za_ref, bias_ref, u_ref, *rest, n_pages):
    del pt_ref
    k_pages = rest[:n_pages]
    v_pages = rest[n_pages:2 * n_pages]
    o_ref, acc_ref, carry_ref, qbd_ref = rest[2 * n_pages:]
    c = pl.program_id(1)
    n_q = q_ref.shape[1]
    rows = n_q * N_HEADS
    nt_dims = (((1,), (1,)), ((), ()))
    bias = bias_ref[...]

    @pl.when(c == 0)
    def _():
        q = q_ref[0].astype(F32)
        head_of_lane = lax.broadcasted_iota(jnp.int32, (N_HEADS, ATTN_WIDTH), 1) // HEAD_DIM
        own = head_of_lane == lax.broadcasted_iota(jnp.int32, (N_HEADS, ATTN_WIDTH), 0)
        qbd = jnp.concatenate(
            [jnp.where(own, jnp.broadcast_to(q[i:i + 1], (N_HEADS, ATTN_WIDTH)), 0.0) for i in range(n_q)],
            axis=0).astype(BF16)
        qbd_ref[...] = qbd
        pad = jnp.zeros((LANES - kn_ref.shape[1], ATTN_WIDTH), F32)
        kn = jnp.concatenate([kn_ref[0], pad], axis=0).astype(BF16)
        vn = jnp.concatenate([vn_ref[0], pad], axis=0).astype(BF16)
        z = lax.dot_general(qbd, kn, nt_dims, preferred_element_type=F32) + bias
        q_idx = lax.broadcasted_iota(jnp.int32, (rows, LANES), 0) // N_HEADS
        visible = lax.broadcasted_iota(jnp.int32, (rows, LANES), 1) < q_idx
        l = jnp.where(visible, _softplus(z), 0.0)
        l_wide = jnp.concatenate([l, jnp.zeros((rows, u_ref.shape[1] - LANES), F32)], axis=1)
        logw = z - l - _suffix_sums(l_wide, u_ref[...])[:, 0:LANES]
        w = jnp.where(visible, jnp.exp(logw), 0.0)
        acc_ref[...] = jnp.dot(w.astype(BF16), vn, preferred_element_type=F32)
        carry_ref[...] = jnp.sum(l, axis=1, keepdims=True)

    page_t = lambda ref: ref[...].reshape(ATTN_WIDTH, LANES).astype(BF16)
    kt = jnp.concatenate([page_t(r) for r in k_pages], axis=1)
    z = jnp.dot(qbd_ref[...], kt, preferred_element_type=F32) + bias
    l = _softplus(z)
    blk = u_ref.shape[1]
    n_blk = n_pages * LANES // blk
    l_blocks = [l[:, p * blk:(p + 1) * blk] for p in range(n_blk)]
    cs_all = _suffix_sums(jnp.concatenate(l_blocks, axis=0), u_ref[...])
    carry = carry_ref[...]
    w_blocks = [None] * n_blk
    for p in reversed(range(n_blk)):
        cs = cs_all[p * rows:(p + 1) * rows]
        zb = z[:, p * blk:(p + 1) * blk]
        w_blocks[p] = jnp.exp(zb - l_blocks[p] - cs - carry).astype(BF16)
        carry = carry + cs[:, 0:1] + l_blocks[p][:, 0:1]
    carry_ref[...] = carry
    vt = jnp.concatenate([page_t(r) for r in v_pages], axis=1)
    acc = acc_ref[...] + lax.dot_general(jnp.concatenate(w_blocks, axis=1), vt, nt_dims,
                                         preferred_element_type=F32)
    acc_ref[...] = acc

    @pl.when(c == pl.num_programs(1) - 1)
    def _():
        head_of_lane = lax.broadcasted_iota(jnp.int32, (rows, ATTN_WIDTH), 1) // HEAD_DIM
        own = head_of_lane == lax.broadcasted_iota(jnp.int32, (rows, ATTN_WIDTH), 0) % N_HEADS
        picked = jnp.where(own, acc, 0.0).reshape(n_q, N_HEADS, ATTN_WIDTH).sum(axis=1)
        za = za_ref[0]
        o_ref[0] = (picked * (za * jax.nn.sigmoid(za))).astype(BF16)


def _sample_attn(page_table, q, kn, vn, za, bias_rows, u, pool_kt, pool_vt):
    n_b, n_q, _ = q.shape
    n_logical = page_table.shape[1]
    npg = PAGES_PER_STEP
    n_chunks = n_logical // npg
    pt_flat = page_table.reshape(-1)
    per_b = lambda w: pl.BlockSpec((1, w, ATTN_WIDTH), lambda b, c, pt: (b, 0, 0))

    def page_spec(i):
        def index(b, c, pt):
            return (pt[b * n_logical + n_logical - npg * (c + 1) + i], 0, 0, 0)
        return pl.BlockSpec((None, N_HEADS, HEAD_DIM, LANES), index)

    rows = n_q * N_HEADS
    const = lambda shape: pl.BlockSpec(shape, lambda b, c, pt: (0,) * len(shape))
    grid_spec = pltpu.PrefetchScalarGridSpec(
        num_scalar_prefetch=1,
        grid=(n_b, n_chunks),
        in_specs=[per_b(n_q), per_b(kn.shape[1]), per_b(vn.shape[1]), per_b(n_q),
                  const(bias_rows.shape), const(u.shape)]
        + [page_spec(i) for i in range(npg)] * 2,
        out_specs=per_b(n_q),
        scratch_shapes=[pltpu.VMEM((rows, ATTN_WIDTH), F32), pltpu.VMEM((rows, 1), F32),
                        pltpu.VMEM((rows, ATTN_WIDTH), BF16)],
    )
    return pl.pallas_call(
        functools.partial(_sample_attn_kernel, n_pages=npg),
        grid_spec=grid_spec,
        out_shape=jax.ShapeDtypeStruct((n_b, n_q, ATTN_WIDTH), BF16),
        compiler_params=pltpu.CompilerParams(
            dimension_semantics=("parallel", "arbitrary"), vmem_limit_bytes=VMEM_LIMIT),
        name="sample_attn",
    )(pt_flat, q, kn, vn, za, bias_rows, u, *([pool_kt] * npg), *([pool_vt] * npg))


def _out_proj_kernel(x_ref, ms_ref, ma_ref, wo_ref, gain_ref, y_ref):
    h = (x_ref[...]
         + jnp.dot(ms_ref[...], wo_ref[0:SSM_WIDTH, :], preferred_element_type=F32)
         + jnp.dot(ma_ref[...], wo_ref[SSM_WIDTH:, :], preferred_element_type=F32))
    inv = lax.rsqrt(jnp.mean(h * h, axis=-1, keepdims=True) + RMS_EPS)
    y_ref[...] = h * inv * gain_ref[...]


def _out_proj(x2, ms, ma, wo, gain):
    n_rows, d_model = x2.shape
    tr = min(OUT_ROWS, n_rows)
    rows = lambda w: pl.BlockSpec((tr, w), lambda i: (i, 0))
    return pl.pallas_call(
        _out_proj_kernel,
        grid=(n_rows // tr,),
        in_specs=[rows(d_model), rows(SSM_WIDTH), rows(ATTN_WIDTH), _const_spec(wo.shape),
                  _const_spec(gain.shape)],
        out_specs=rows(d_model),
        out_shape=jax.ShapeDtypeStruct((n_rows, d_model), F32),
        compiler_params=pltpu.CompilerParams(
            dimension_semantics=("parallel",), vmem_limit_bytes=VMEM_LIMIT),
        name="out_proj",
    )(x2, ms, ma, wo, gain)


def kernel(x_prompt, x_sample, cache_k, cache_v, state_ssm_re, state_ssm_im, page_table, norm_gain, w_in, sb_bias, lambda_re, lambda_im, log_dt, b_re, b_im, c_re, c_im, d_skip, w_glu, b_glu, w_out, final_norm_gain):
    assert w_in.shape[0] == 1, "single-layer trunk only"
    n_b, t, d_model = x_prompt.shape
    n_s, n_q, _ = x_sample.shape
    assert (d_model, w_in.shape[2]) == (1024, 2 * SSM_WIDTH + 4 * ATTN_WIDTH)
    assert n_b % PROMPT_NB == 0 and t % PROMPT_LT == 0 and t % ATTN_TQ == 0 and n_s % SUBLANES == 0

    a_re, a_im, bb_re, bb_im = _discretize(lambda_re[0], lambda_im[0], log_dt[0], b_re[0], b_im[0])
    bt, ct = _block_diag_weights(bb_re, bb_im, c_re[0], c_im[0])
    w = w_in[0].astype(BF16)
    cut = 2 * SSM_WIDTH
    wa = jnp.concatenate([w[:, :cut + ATTN_WIDTH], w[:, cut + 3 * ATTN_WIDTH:]], axis=1)
    wkt = w[:, cut + ATTN_WIDTH:cut + 2 * ATTN_WIDTH].T
    wvt = w[:, cut + 2 * ATTN_WIDTH:cut + 3 * ATTN_WIDTH].T
    row = lambda a: a.astype(F32).reshape(1, -1)
    weights = (row(norm_gain[0]), wa, wkt, wvt, bt, ct, a_re, a_im, row(d_skip[0]),
               w_glu[0].astype(BF16), row(b_glu[0]))
    wo = w_out[0].astype(BF16)
    gain_f = row(final_norm_gain)
    tri = lax.broadcasted_iota(jnp.int32, (ATTN_TQ, ATTN_TQ), 0) > lax.broadcasted_iota(jnp.int32, (ATTN_TQ, ATTN_TQ), 1)
    u = jnp.concatenate([tri, tri], axis=0).astype(BF16)

    zeros = jnp.zeros((n_b, N_STATE), F32)
    pitch = PROMPT_LT + SUBLANES
    kt_p, vt_p, q_p, za_p, ms_p, hre_p, him_p = _trunk_in(
        x_prompt, weights, zeros, zeros, nb=PROMPT_NB, lt=PROMPT_LT, pitch=pitch,
        x_block=(PROMPT_NB, PROMPT_LT), kt_block=(PROMPT_NB, ATTN_WIDTH, PROMPT_LT),
        kt_shape=(n_b, ATTN_WIDTH, t))
    ma_p = _prompt_attn(sb_bias[0].astype(F32), q_p, kt_p, vt_p, za_p, u)
    y_prompt = _out_proj(x_prompt.reshape(n_b * t, d_model), ms_p.reshape(n_b * t, SSM_WIDTH),
                         ma_p.reshape(n_b * t, ATTN_WIDTH), wo, gain_f).reshape(n_b, t, d_model)
    heads_out = lambda a: a.reshape(1, n_b, N_HEADS, HEAD_DIM, t).transpose(0, 1, 4, 2, 3)
    state_out = lambda a, n: a.reshape(1, n, N_SSM_GROUPS, SSM_STATE)

    rows_s = n_s * n_q
    kt_s, vt_s, q_s, za_s, ms_s, hre_s, him_s = _trunk_in(
        x_sample.reshape(1, rows_s, d_model), weights,
        state_ssm_re[0].reshape(n_s, N_STATE).astype(F32), state_ssm_im[0].reshape(n_s, N_STATE).astype(F32),
        nb=n_s, lt=n_q, pitch=n_q, x_block=(1, rows_s), kt_block=(1, ATTN_WIDTH, rows_s),
        kt_shape=(1, ATTN_WIDTH, rows_s))
    new_rows = lambda a: a[0].T.reshape(n_s, n_q, ATTN_WIDTH)
    k_new, v_new = new_rows(kt_s), new_rows(vt_s)
    pad8 = lambda a: jnp.pad(a, ((0, 0), (0, SUBLANES - n_q), (0, 0)))
    bias_rows = jnp.tile(sb_bias[0].astype(F32), n_q).reshape(n_q * N_HEADS, 1)
    pool_t = lambda c: jnp.transpose(c[0], (0, 2, 3, 1))
    ma_s = _sample_attn(page_table, q_s.reshape(n_s, n_q, ATTN_WIDTH), pad8(k_new), pad8(v_new),
                        za_s.reshape(n_s, n_q, ATTN_WIDTH), bias_rows, u, pool_t(cache_k), pool_t(cache_v))
    y_sample = _out_proj(x_sample.reshape(rows_s, d_model), ms_s.reshape(rows_s, SSM_WIDTH),
                         ma_s.reshape(rows_s, ATTN_WIDTH), wo, gain_f).reshape(n_s, n_q, d_model)
    new_out = lambda a: a.reshape(1, n_s, n_q, N_HEADS, HEAD_DIM)

    return (y_prompt, y_sample,
            heads_out(kt_p), heads_out(vt_p), state_out(hre_p, n_b), state_out(him_p, n_b),
            new_out(k_new), new_out(v_new), state_out(hre_s, n_s), state_out(him_s, n_s))
```

```python
import functools
import math

import jax
import jax.numpy as jnp
from jax import lax
from jax.experimental import pallas as pl
from jax.experimental.pallas import tpu as pltpu

F32 = jnp.float32
BF16 = jnp.bfloat16

N_HEADS = 8
HEAD_DIM = 64
ATTN_WIDTH = N_HEADS * HEAD_DIM
SSM_WIDTH = 512
SSM_GROUP = 16
N_SSM_GROUPS = SSM_WIDTH // SSM_GROUP
SSM_STATE = 64
N_STATE = N_SSM_GROUPS * SSM_STATE
HALF_STATE = N_STATE // 2
HALF_CH = SSM_WIDTH // 2
ATTN_SCALE = HEAD_DIM ** -0.5
RMS_EPS = 1e-6
LOG2E = math.log2(math.e)

LANES = 128
SUBLANES = 8
SLABS_PER_HALF = 2 * HALF_STATE // LANES
VMEM_LIMIT = 56 * 1024 * 1024

PROMPT_NB = 8
PROMPT_LT = 128
ATTN_TQ = 256
PAGES_PER_STEP = 32
OUT_ROWS = 1024


def _const_spec(shape):
    nd = len(shape)
    return pl.BlockSpec(shape, lambda *_: (0,) * nd, pipeline_mode=pl.Buffered(1))


def _softplus(z):
    return jnp.maximum(z, 0.0) + jnp.log(1.0 + jnp.exp2(jnp.abs(z) * (-LOG2E)))


def _suffix_sums(l, u2):
    l_hi = l.astype(BF16)
    l_lo = (l - l_hi.astype(F32)).astype(BF16)
    return jnp.dot(jnp.concatenate([l_hi, l_lo], axis=1), u2, preferred_element_type=F32)


def _suffix_sums_bf16(l, u):
    return jnp.dot(l.astype(BF16), u, preferred_element_type=F32)


def _discretize_kernel(lre_ref, lim_ref, ldt_ref, bre_ref, bim_ref,
                       are_ref, aim_ref, bbre_ref, bbim_ref):
    lr = lre_ref[...]
    li = lim_ref[...]
    dt = jnp.exp(ldt_ref[...])
    mag = jnp.exp(lr * dt)
    a_re = mag * jnp.cos(li * dt)
    a_im = mag * jnp.sin(li * dt)
    den = lr * lr + li * li
    g_re = ((a_re - 1.0) * lr + a_im * li) / den
    g_im = (a_im * lr - (a_re - 1.0) * li) / den
    br = bre_ref[...]
    bi = bim_ref[...]
    are_ref[...] = a_re
    aim_ref[...] = a_im
    bbre_ref[...] = g_re * br - g_im * bi
    bbim_ref[...] = g_re * bi + g_im * br


def _discretize(lam_re, lam_im, log_dt, b_re, b_im):
    col = lambda a: a.astype(F32).reshape(N_STATE, 1)
    ldt = jnp.broadcast_to(log_dt.astype(F32)[:, None], (N_SSM_GROUPS, SSM_STATE))
    outs = pl.pallas_call(
        _discretize_kernel,
        out_shape=[jax.ShapeDtypeStruct((N_STATE, 1), F32)] * 2
        + [jax.ShapeDtypeStruct((N_STATE, SSM_GROUP), F32)] * 2,
        name="discretize",
    )(col(lam_re), col(lam_im), col(ldt),
      b_re.astype(F32).reshape(N_STATE, SSM_GROUP), b_im.astype(F32).reshape(N_STATE, SSM_GROUP))
    a_re, a_im, bb_re, bb_im = outs
    shape3 = (N_SSM_GROUPS, SSM_STATE, SSM_GROUP)
    return a_re.reshape(1, N_STATE), a_im.reshape(1, N_STATE), bb_re.reshape(shape3), bb_im.reshape(shape3)


def _block_diag_weights(bb_re, bb_im, c_re, c_im):
    gh = N_SSM_GROUPS // 2
    eye = jnp.eye(gh, dtype=F32)
    bts, cts = [], []
    for hf in range(2):
        sl = slice(hf * gh, (hf + 1) * gh)
        bd_in = lambda b: jnp.einsum('gnc,gh->gchn', b[sl], eye).reshape(HALF_CH, HALF_STATE)
        bts.append(jnp.concatenate([bd_in(bb_re), bd_in(bb_im)], axis=1))
        bd_out = lambda c: jnp.einsum('gcn,gh->hngc', c[sl], eye).reshape(HALF_STATE, HALF_CH)
        cts.append(jnp.concatenate([bd_out(c_re.astype(F32)), -bd_out(c_im.astype(F32))], axis=0))
    return jnp.stack(bts).astype(BF16), jnp.stack(cts).astype(BF16)


def _trunk_in_kernel(x_ref, gain_ref, wa_ref, wkt_ref, wvt_ref, bt_ref, ct_ref, are_ref, aim_ref,
                     dskip_ref, wglu_ref, bglu_ref, h0re_ref, h0im_ref,
                     kt_ref, vt_ref, q_ref, za_ref, ms_ref, hre_ref, him_ref,
                     s_ref, *, nb, lt, pitch):
    r = nb * lt
    d_model = x_ref.shape[-1]
    tc = pl.program_id(1)

    @pl.when(tc == 0)
    def _():
        hre_ref[...] = h0re_ref[...]
        him_ref[...] = h0im_ref[...]

    x = x_ref[...].reshape(r, d_model)
    inv = lax.rsqrt(jnp.mean(x * x, axis=-1, keepdims=True) + RMS_EPS)
    xn = (x * inv * gain_ref[...]).astype(BF16)

    def proj(c):
        return jnp.dot(xn, wa_ref[:, c * 512:(c + 1) * 512], preferred_element_type=F32)

    nt_dims = (((1,), (1,)), ((), ()))
    ka, _, kl = kt_ref.shape

    def project_kv(w_ref, o_ref):
        t = lax.dot_general(w_ref[...], xn, nt_dims, preferred_element_type=F32)
        for a in range(ka):
            o_ref[a] = t[:, a * kl:(a + 1) * kl]

    def project_q_za():
        q_ref[...] = (proj(2) * ATTN_SCALE).astype(BF16).reshape(q_ref.shape)
        za_ref[...] = proj(3).reshape(za_ref.shape)

    between_stages = [lambda: (project_kv(wkt_ref, kt_ref), project_kv(wvt_ref, vt_ref)), project_q_za]
    u = proj(0)
    u_bf = u.astype(BF16)

    def slab_rows(slab):
        if pitch == lt:
            return s_ref[slab, 0:r, :]
        return jnp.concatenate([s_ref[slab, b * pitch:b * pitch + lt, :] for b in range(nb)], axis=0)

    def scan_group(hf, g):
        row0 = g * (SUBLANES * pitch)
        bsl = pl.ds(g * SUBLANES if isinstance(g, int) else pl.multiple_of(g * SUBLANES, SUBLANES), SUBLANES)
        for cc in range(2):
            lane0 = hf * HALF_STATE + cc * 512
            lanes = [slice(lane0 + i * LANES, lane0 + (i + 1) * LANES) for i in range(4)]
            ar = [jnp.broadcast_to(are_ref[:, ls], (SUBLANES, LANES)) for ls in lanes]
            ai = [jnp.broadcast_to(aim_ref[:, ls], (SUBLANES, LANES)) for ls in lanes]
            hr0 = tuple(hre_ref[bsl, ls] for ls in lanes)
            hi0 = tuple(him_ref[bsl, ls] for ls in lanes)

            def step(t, carry):
                hr, hi = carry
                rows = pl.ds(row0 + t, SUBLANES, stride=pitch)
                nr, ni = [], []
                for i in range(4):
                    sre = cc * 4 + i
                    sim = SLABS_PER_HALF // 2 + cc * 4 + i
                    h_re = ar[i] * hr[i] - ai[i] * hi[i] + s_ref[sre, rows, :]
                    h_im = ar[i] * hi[i] + ai[i] * hr[i] + s_ref[sim, rows, :]
                    s_ref[sre, rows, :] = h_re
                    s_ref[sim, rows, :] = h_im
                    nr.append(h_re)
                    ni.append(h_im)
                return tuple(nr), tuple(ni)

            hr, hi = hr0, hi0
            for t in range(lt):
                hr, hi = step(t, (hr, hi))
            for i, ls in enumerate(lanes):
                hre_ref[bsl, ls] = hr[i]
                him_ref[bsl, ls] = hi[i]

    y_halves = []
    for hf in range(2):
        u_half = u_bf[:, hf * HALF_CH:(hf + 1) * HALF_CH]
        for c in range(4):
            bu = jnp.dot(u_half, bt_ref[hf, :, c * 512:(c + 1) * 512], preferred_element_type=F32)
            for i in range(4):
                piece = bu[:, i * LANES:(i + 1) * LANES]
                if pitch == lt:
                    s_ref[c * 4 + i, 0:r, :] = piece
                else:
                    for b in range(nb):
                        s_ref[c * 4 + i, b * pitch:b * pitch + lt, :] = piece[b * lt:(b + 1) * lt]
        between_stages[hf]()
        if nb == SUBLANES:
            scan_group(hf, 0)
        else:
            def group_body(g, carry, hf=hf):
                scan_group(hf, g)
                return carry
            lax.fori_loop(0, nb // SUBLANES, group_body, 0)
        y_h = jnp.zeros((r, HALF_CH), F32)
        for kc in range(4):
            hcat = jnp.concatenate([slab_rows(kc * 4 + i) for i in range(4)], axis=1).astype(BF16)
            y_h = y_h + jnp.dot(hcat, ct_ref[hf, kc * 512:(kc + 1) * 512, :], preferred_element_type=F32)
        y_halves.append(y_h)

    y = jnp.concatenate(y_halves, axis=1) + dskip_ref[...] * u
    y = jax.nn.gelu(y)
    gate = jax.nn.sigmoid(jnp.dot(y.astype(BF16), wglu_ref[...], preferred_element_type=F32) + bglu_ref[...])
    zs = proj(1)
    ms_ref[...] = (y * gate * (zs * jax.nn.sigmoid(zs))).astype(BF16).reshape(ms_ref.shape)


def _trunk_in(x3, weights, h0_re, h0_im, *, nb, lt, pitch, x_block, kt_block, kt_shape):
    (gain, wa, wkt, wvt, bt, ct, a_re, a_im, dskip, wglu, bglu) = weights
    xa, xr, d_model = x3.shape
    n_seq = h0_re.shape[0]
    grid = (n_seq // nb, (xa * xr) // (n_seq * lt))
    n_tc = grid[1]
    row_spec = lambda w: pl.BlockSpec((x_block[0], x_block[1], w), lambda g, t: (g, t, 0))
    state_spec = pl.BlockSpec((nb, N_STATE), lambda g, t: (g, 0))
    kt_spec = pl.BlockSpec(kt_block, lambda g, t: (g, 0, t))
    del n_tc
    out_shape = [
        jax.ShapeDtypeStruct(kt_shape, F32), jax.ShapeDtypeStruct(kt_shape, F32),
        jax.ShapeDtypeStruct((xa, xr, ATTN_WIDTH), BF16),
        jax.ShapeDtypeStruct((xa, xr, ATTN_WIDTH), F32),
        jax.ShapeDtypeStruct((xa, xr, SSM_WIDTH), BF16),
        jax.ShapeDtypeStruct((n_seq, N_STATE), F32), jax.ShapeDtypeStruct((n_seq, N_STATE), F32),
    ]
    return pl.pallas_call(
        functools.partial(_trunk_in_kernel, nb=nb, lt=lt, pitch=pitch),
        grid=grid,
        in_specs=[row_spec(d_model), _const_spec(gain.shape), _const_spec(wa.shape),
                  _const_spec(wkt.shape), _const_spec(wvt.shape), _const_spec(bt.shape),
                  _const_spec(ct.shape), _const_spec(a_re.shape), _const_spec(a_im.shape),
                  _const_spec(dskip.shape), _const_spec(wglu.shape), _const_spec(bglu.shape),
                  state_spec, state_spec],
        out_specs=[kt_spec, kt_spec, row_spec(ATTN_WIDTH), row_spec(ATTN_WIDTH), row_spec(SSM_WIDTH),
                   state_spec, state_spec],
        out_shape=out_shape,
        scratch_shapes=[pltpu.VMEM((SLABS_PER_HALF, nb * pitch, LANES), F32)],
        compiler_params=pltpu.CompilerParams(
            dimension_semantics=("parallel", "arbitrary"), vmem_limit_bytes=VMEM_LIMIT),
        name="trunk_in",
    )(x3, gain, wa, wkt, wvt, bt, ct, a_re, a_im, dskip, wglu, bglu, h0_re, h0_im)


def _prompt_attn_kernel(bias_ref, q_ref, kt_ref, vt_ref, za_ref, u_ref, o_ref,
                        ktb_ref, vtb_ref, qh_ref, acc_ref, carry_ref, *, tq):
    qi = pl.program_id(1)
    n_blk = ktb_ref.shape[0]
    nt_dims = (((1,), (1,)), ((), ()))

    @pl.when(qi == 0)
    def _():
        aug_row = lax.broadcasted_iota(jnp.int32, (HEAD_DIM, tq), 0)
        zeros = jnp.zeros((HEAD_DIM, tq), BF16)
        for h in range(N_HEADS):
            hd = slice(h * HEAD_DIM, (h + 1) * HEAD_DIM)
            b = jnp.full((HEAD_DIM, tq), bias_ref[h], F32)
            b_hi = b.astype(BF16).astype(F32)
            b_mid = (b - b_hi).astype(BF16).astype(F32)
            b_lo = b - b_hi - b_mid
            aug = jnp.where(aug_row == 0, b_hi, jnp.where(aug_row == 1, b_mid,
                                                         jnp.where(aug_row == 2, b_lo, 0.0))).astype(BF16)
            for j in range(n_blk):
                keys = slice(j * tq, (j + 1) * tq)
                ktb_ref[j, h, 0:HEAD_DIM, :] = kt_ref[0, hd, keys].astype(BF16)
                ktb_ref[j, h, HEAD_DIM:, :] = aug
                vt = vt_ref[0, hd, keys].astype(BF16)
                vtb_ref[j, h, 0:HEAD_DIM, :] = vt if h % 2 == 0 else zeros
                vtb_ref[j, h, HEAD_DIM:, :] = zeros if h % 2 == 0 else vt

    ones = (lax.broadcasted_iota(jnp.int32, (tq, HEAD_DIM), 1) < 3).astype(F32).astype(BF16)
    for h in range(N_HEADS):
        qh_ref[h] = jnp.concatenate([q_ref[0, :, h * HEAD_DIM:(h + 1) * HEAD_DIM], ones], axis=1)

    row = lax.broadcasted_iota(jnp.int32, (tq, tq), 0)
    col = lax.broadcasted_iota(jnp.int32, (tq, tq), 1)
    visible = col < row
    u = u_ref[0:tq, :]

    def block(kb, diagonal):
        zs, ls, css, ws, pvs = ({} for _ in range(5))

        def run(stage, h):
            if stage == 0:
                zs[h] = jnp.dot(qh_ref[h], ktb_ref[kb, h], preferred_element_type=F32)
            elif stage == 1:
                l = _softplus(zs[h])
                ls[h] = jnp.where(visible, l, 0.0) if diagonal else l
            elif stage == 2:
                css[h] = _suffix_sums_bf16(ls[h], u)
            elif stage == 3:
                total = css[h][:, 0:1] + ls[h][:, 0:1]
                if diagonal:
                    ws[h] = jnp.where(visible, jnp.exp(zs[h] - ls[h] - css[h]), 0.0).astype(BF16)
                    carry_ref[h] = total
                else:
                    carry = carry_ref[h]
                    ws[h] = jnp.exp(zs[h] - ls[h] - css[h] - carry).astype(BF16)
                    carry_ref[h] = carry + total
            else:
                pvs[h] = lax.dot_general(ws[h], vtb_ref[kb, h], nt_dims, preferred_element_type=F32)
                if h % 2 == 1:
                    pair = pvs[h - 1] + pvs[h]
                    if diagonal:
                        acc_ref[h // 2] = pair
                    else:
                        acc_ref[h // 2] += pair

        n_stage = 5
        for tick in range(N_HEADS + n_stage - 1):
            for stage in reversed(range(n_stage)):
                if 0 <= tick - stage < N_HEADS:
                    run(stage, tick - stage)

    block(qi, True)

    def body(j, c):
        block(qi - 1 - j, False)
        return c
    lax.fori_loop(0, qi, body, 0)

    za = za_ref[0]
    o = jnp.concatenate([acc_ref[p] for p in range(N_HEADS // 2)], axis=1)
    o_ref[0] = (o * (za * jax.nn.sigmoid(za))).astype(BF16)


def _prompt_attn(sb_bias, q, kt, vt, za, u):
    n_b, t, _ = q.shape
    tq = ATTN_TQ
    q_spec = pl.BlockSpec((1, tq, ATTN_WIDTH), lambda b, i: (b, i, 0))
    kv_spec = pl.BlockSpec((1, ATTN_WIDTH, t), lambda b, i: (b, 0, 0))
    return pl.pallas_call(
        functools.partial(_prompt_attn_kernel, tq=tq),
        grid=(n_b, t // tq),
        in_specs=[pl.BlockSpec(memory_space=pltpu.SMEM), q_spec, kv_spec, kv_spec, q_spec,
                  _const_spec(u.shape)],
        out_specs=q_spec,
        out_shape=jax.ShapeDtypeStruct((n_b, t, ATTN_WIDTH), BF16),
        scratch_shapes=[pltpu.VMEM((t // tq, N_HEADS, 2 * HEAD_DIM, tq), BF16)] * 2
        + [pltpu.VMEM((N_HEADS, tq, 2 * HEAD_DIM), BF16), pltpu.VMEM((N_HEADS // 2, tq, 2 * HEAD_DIM), F32),
           pltpu.VMEM((N_HEADS, tq, 1), F32)],
        compiler_params=pltpu.CompilerParams(
            dimension_semantics=("parallel", "arbitrary"), vmem_limit_bytes=VMEM_LIMIT),
        name="prompt_attn",
    )(sb_bias, q, kt, vt, za, u)


def _sample_attn_kernel(pt_ref, q_ref, kn_ref, vn_ref, za_ref, bias_ref, u_ref, *rest, n_pages):
    del pt_ref
    k_pages = rest[:n_pages]
    v_pages = rest[n_pages:2 * n_pages]
    o_ref, acc_ref, carry_ref, qbd_ref = rest[2 * n_pages:]
    c = pl.program_id(1)
    n_q = q_ref.shape[1]
    rows = n_q * N_HEADS
    nt_dims = (((1,), (1,)), ((), ()))
    bias = bias_ref[...]

    @pl.when(c == 0)
    def _():
        q = q_ref[0].astype(F32)
        head_of_lane = lax.broadcasted_iota(jnp.int32, (N_HEADS, ATTN_WIDTH), 1) // HEAD_DIM
        own = head_of_lane == lax.broadcasted_iota(jnp.int32, (N_HEADS, ATTN_WIDTH), 0)
        qbd = jnp.concatenate(
            [jnp.where(own, jnp.broadcast_to(q[i:i + 1], (N_HEADS, ATTN_WIDTH)), 0.0) for i in range(n_q)],
            axis=0).astype(BF16)
        qbd_ref[...] = qbd
        pad = jnp.zeros((LANES - kn_ref.shape[1], ATTN_WIDTH), F32)
        kn = jnp.concatenate([kn_ref[0], pad], axis=0).astype(BF16)
        vn = jnp.concatenate([vn_ref[0], pad], axis=0).astype(BF16)
        z = lax.dot_general(qbd, kn, nt_dims, preferred_element_type=F32) + bias
        q_idx = lax.broadcasted_iota(jnp.int32, (rows, LANES), 0) // N_HEADS
        visible = lax.broadcasted_iota(jnp.int32, (rows, LANES), 1) < q_idx
        l = jnp.where(visible, _softplus(z), 0.0)
        l_wide = jnp.concatenate([l, jnp.zeros((rows, u_ref.shape[1] - LANES), F32)], axis=1)
        logw = z - l - _suffix_sums(l_wide, u_ref[...])[:, 0:LANES]
        w = jnp.where(visible, jnp.exp(logw), 0.0)
        acc_ref[...] = jnp.dot(w.astype(BF16), vn, preferred_element_type=F32)
        carry_ref[...] = jnp.sum(l, axis=1, keepdims=True)

    page_t = lambda ref: ref[...].reshape(ATTN_WIDTH, LANES).astype(BF16)
    kt = jnp.concatenate([page_t(r) for r in k_pages], axis=1)
    z = jnp.dot(qbd_ref[...], kt, preferred_element_type=F32) + bias
    l = _softplus(z)
    blk = u_ref.shape[1]
    n_blk = n_pages * LANES // blk
    l_blocks = [l[:, p * blk:(p + 1) * blk] for p in range(n_blk)]
    cs_all = _suffix_sums(jnp.concatenate(l_blocks, axis=0), u_ref[...])
    carry = carry_ref[...]
    w_blocks = [None] * n_blk
    for p in reversed(range(n_blk)):
        cs = cs_all[p * rows:(p + 1) * rows]
        zb = z[:, p * blk:(p + 1) * blk]
        w_blocks[p] = jnp.exp(zb - l_blocks[p] - cs - carry).astype(BF16)
        carry = carry + cs[:, 0:1] + l_blocks[p][:, 0:1]
    carry_ref[...] = carry
    vt = jnp.concatenate([page_t(r) for r in v_pages], axis=1)
    acc = acc_ref[...] + lax.dot_general(jnp.concatenate(w_blocks, axis=1), vt, nt_dims,
                                         preferred_element_type=F32)
    acc_ref[...] = acc

    @pl.when(c == pl.num_programs(1) - 1)
    def _():
        head_of_lane = lax.broadcasted_iota(jnp.int32, (rows, ATTN_WIDTH), 1) // HEAD_DIM
        own = head_of_lane == lax.broadcasted_iota(jnp.int32, (rows, ATTN_WIDTH), 0) % N_HEADS
        picked = jnp.where(own, acc, 0.0).reshape(n_q, N_HEADS, ATTN_WIDTH).sum(axis=1)
        za = za_ref[0]
        o_ref[0] = (picked * (za * jax.nn.sigmoid(za))).astype(BF16)


def _sample_attn(page_table, q, kn, vn, za, bias_rows, u, pool_kt, pool_vt):
    n_b, n_q, _ = q.shape
    n_logical = page_table.shape[1]
    npg = PAGES_PER_STEP
    n_chunks = n_logical // npg
    pt_flat = page_table.reshape(-1)
    per_b = lambda w: pl.BlockSpec((1, w, ATTN_WIDTH), lambda b, c, pt: (b, 0, 0))

    def page_spec(i):
        def index(b, c, pt):
            return (pt[b * n_logical + n_logical - npg * (c + 1) + i], 0, 0, 0)
        return pl.BlockSpec((None, N_HEADS, HEAD_DIM, LANES), index)

    rows = n_q * N_HEADS
    const = lambda shape: pl.BlockSpec(shape, lambda b, c, pt: (0,) * len(shape))
    grid_spec = pltpu.PrefetchScalarGridSpec(
        num_scalar_prefetch=1,
        grid=(n_b, n_chunks),
        in_specs=[per_b(n_q), per_b(kn.shape[1]), per_b(vn.shape[1]), per_b(n_q),
                  const(bias_rows.shape), const(u.shape)]
        + [page_spec(i) for i in range(npg)] * 2,
        out_specs=per_b(n_q),
        scratch_shapes=[pltpu.VMEM((rows, ATTN_WIDTH), F32), pltpu.VMEM((rows, 1), F32),
                        pltpu.VMEM((rows, ATTN_WIDTH), BF16)],
    )
    return pl.pallas_call(
        functools.partial(_sample_attn_kernel, n_pages=npg),
        grid_spec=grid_spec,
        out_shape=jax.ShapeDtypeStruct((n_b, n_q, ATTN_WIDTH), BF16),
        compiler_params=pltpu.CompilerParams(
            dimension_semantics=("parallel", "arbitrary"), vmem_limit_bytes=VMEM_LIMIT),
        name="sample_attn",
    )(pt_flat, q, kn, vn, za, bias_rows, u, *([pool_kt] * npg), *([pool_vt] * npg))


def _out_proj_kernel(x_ref, ms_ref, ma_ref, wo_ref, gain_ref, y_ref):
    h = (x_ref[...]
         + jnp.dot(ms_ref[...], wo_ref[0:SSM_WIDTH, :], preferred_element_type=F32)
         + jnp.dot(ma_ref[...], wo_ref[SSM_WIDTH:, :], preferred_element_type=F32))
    inv = lax.rsqrt(jnp.mean(h * h, axis=-1, keepdims=True) + RMS_EPS)
    y_ref[...] = h * inv * gain_ref[...]


def _out_proj(x2, ms, ma, wo, gain):
    n_rows, d_model = x2.shape
    tr = min(OUT_ROWS, n_rows)
    rows = lambda w: pl.BlockSpec((tr, w), lambda i: (i, 0))
    return pl.pallas_call(
        _out_proj_kernel,
        grid=(n_rows // tr,),
        in_specs=[rows(d_model), rows(SSM_WIDTH), rows(ATTN_WIDTH), _const_spec(wo.shape),
                  _const_spec(gain.shape)],
        out_specs=rows(d_model),
        out_shape=jax.ShapeDtypeStruct((n_rows, d_model), F32),
        compiler_params=pltpu.CompilerParams(
            dimension_semantics=("parallel",), vmem_limit_bytes=VMEM_LIMIT),
        name="out_proj",
    )(x2, ms, ma, wo, gain)


def kernel(x_prompt, x_sample, cache_k, cache_v, state_ssm_re, state_ssm_im, page_table, norm_gain, w_in, sb_bias, lambda_re, lambda_im, log_dt, b_re, b_im, c_re, c_im, d_skip, w_glu, b_glu, w_out, final_norm_gain):
    assert w_in.shape[0] == 1, "single-layer trunk only"
    n_b, t, d_model = x_prompt.shape
    n_s, n_q, _ = x_sample.shape
    assert (d_model, w_in.shape[2]) == (1024, 2 * SSM_WIDTH + 4 * ATTN_WIDTH)
    assert n_b % PROMPT_NB == 0 and t % PROMPT_LT == 0 and t % ATTN_TQ == 0 and n_s % SUBLANES == 0

    a_re, a_im, bb_re, bb_im = _discretize(lambda_re[0], lambda_im[0], log_dt[0], b_re[0], b_im[0])
    bt, ct = _block_diag_weights(bb_re, bb_im, c_re[0], c_im[0])
    w = w_in[0].astype(BF16)
    cut = 2 * SSM_WIDTH
    wa = jnp.concatenate([w[:, :cut + ATTN_WIDTH], w[:, cut + 3 * ATTN_WIDTH:]], axis=1)
    wkt = w[:, cut + ATTN_WIDTH:cut + 2 * ATTN_WIDTH].T
    wvt = w[:, cut + 2 * ATTN_WIDTH:cut + 3 * ATTN_WIDTH].T
    row = lambda a: a.astype(F32).reshape(1, -1)
    weights = (row(norm_gain[0]), wa, wkt, wvt, bt, ct, a_re, a_im, row(d_skip[0]),
               w_glu[0].astype(BF16), row(b_glu[0]))
    wo = w_out[0].astype(BF16)
    gain_f = row(final_norm_gain)
    tri = lax.broadcasted_iota(jnp.int32, (ATTN_TQ, ATTN_TQ), 0) > lax.broadcasted_iota(jnp.int32, (ATTN_TQ, ATTN_TQ), 1)
    u = jnp.concatenate([tri, tri], axis=0).astype(BF16)

    zeros = jnp.zeros((n_b, N_STATE), F32)
    pitch = PROMPT_LT + SUBLANES
    kt_p, vt_p, q_p, za_p, ms_p, hre_p, him_p = _trunk_in(
        x_prompt, weights, zeros, zeros, nb=PROMPT_NB, lt=PROMPT_LT, pitch=pitch,
        x_block=(PROMPT_NB, PROMPT_LT), kt_block=(PROMPT_NB, ATTN_WIDTH, PROMPT_LT),
        kt_shape=(n_b, ATTN_WIDTH, t))
    ma_p = _prompt_attn(sb_bias[0].astype(F32), q_p, kt_p, vt_p, za_p, u)
    y_prompt = _out_proj(x_prompt.reshape(n_b * t, d_model), ms_p.reshape(n_b * t, SSM_WIDTH),
                         ma_p.reshape(n_b * t, ATTN_WIDTH), wo, gain_f).reshape(n_b, t, d_model)
    heads_out = lambda a: a.reshape(1, n_b, N_HEADS, HEAD_DIM, t).transpose(0, 1, 4, 2, 3)
    state_out = lambda a, n: a.reshape(1, n, N_SSM_GROUPS, SSM_STATE)

    rows_s = n_s * n_q
    kt_s, vt_s, q_s, za_s, ms_s, hre_s, him_s = _trunk_in(
        x_sample.reshape(1, rows_s, d_model), weights,
        state_ssm_re[0].reshape(n_s, N_STATE).astype(F32), state_ssm_im[0].reshape(n_s, N_STATE).astype(F32),
        nb=n_s, lt=n_q, pitch=n_q, x_block=(1, rows_s), kt_block=(1, ATTN_WIDTH, rows_s),
        kt_shape=(1, ATTN_WIDTH, rows_s))
    new_rows = lambda a: a[0].T.reshape(n_s, n_q, ATTN_WIDTH)
    k_new, v_new = new_rows(kt_s), new_rows(vt_s)
    pad8 = lambda a: jnp.pad(a, ((0, 0), (0, SUBLANES - n_q), (0, 0)))
    bias_rows = jnp.tile(sb_bias[0].astype(F32), n_q).reshape(n_q * N_HEADS, 1)
    pool_t = lambda c: jnp.transpose(c[0], (0, 2, 3, 1))
    ma_s = _sample_attn(page_table, q_s.reshape(n_s, n_q, ATTN_WIDTH), pad8(k_new), pad8(v_new),
                        za_s.reshape(n_s, n_q, ATTN_WIDTH), bias_rows, u, pool_t(cache_k), pool_t(cache_v))
    y_sample = _out_proj(x_sample.reshape(rows_s, d_model), ms_s.reshape(rows_s, SSM_WIDTH),
                         ma_s.reshape(rows_s, ATTN_WIDTH), wo, gain_f).reshape(n_s, n_q, d_model)
    new_out = lambda a: a.reshape(1, n_s, n_q, N_HEADS, HEAD_DIM)

    return (y_prompt, y_sample,
            heads_out(kt_p), heads_out(vt_p), state_out(hre_p, n_b), state_out(him_p, n_b),
            new_out(k_new), new_out(v_new), state_out(hre_s, n_s), state_out(him_s, n_s))
```

```python
import functools
import math

import jax
import jax.numpy as jnp
from jax import lax
from jax.experimental import pallas as pl
from jax.experimental.pallas import tpu as pltpu

F32 = jnp.float32
BF16 = jnp.bfloat16

N_HEADS = 8
HEAD_DIM = 64
ATTN_WIDTH = N_HEADS * HEAD_DIM
SSM_WIDTH = 512
SSM_GROUP = 16
N_SSM_GROUPS = SSM_WIDTH // SSM_GROUP
SSM_STATE = 64
N_STATE = N_SSM_GROUPS * SSM_STATE
N_PART = 2
N_SLOT = 1
CHUNK_COLS = 512
PART_STATE = N_STATE // N_PART
PART_CH = SSM_WIDTH // N_PART
ATTN_SCALE = HEAD_DIM ** -0.5
RMS_EPS = 1e-6
LOG2E = math.log2(math.e)

LANES = 128
SUBLANES = 8
PART_SLABS = 2 * PART_STATE // LANES
VMEM_LIMIT = 56 * 1024 * 1024

PROMPT_NB = 8
PROMPT_LT = 128
ATTN_TQ = 512
ATTN_TK = 256
PAGES_PER_STEP = 32
OUT_ROWS = 2048


def _const_spec(shape):
    nd = len(shape)
    return pl.BlockSpec(shape, lambda *_: (0,) * nd, pipeline_mode=pl.Buffered(1))


def _softplus(z):
    return jnp.maximum(z, 0.0) + jnp.log(1.0 + jnp.exp2(jnp.abs(z) * (-LOG2E)))


def _suffix_sums(l, u2):
    l_hi = l.astype(BF16)
    l_lo = (l - l_hi.astype(F32)).astype(BF16)
    return jnp.dot(jnp.concatenate([l_hi, l_lo], axis=1), u2, preferred_element_type=F32)


def _suffix_sums_bf16(l, u):
    return jnp.dot(l.astype(BF16), u, preferred_element_type=F32)


def _discretize_kernel(lre_ref, lim_ref, ldt_ref, bre_ref, bim_ref,
                       are_ref, aim_ref, bbre_ref, bbim_ref):
    lr = lre_ref[...]
    li = lim_ref[...]
    dt = jnp.exp(ldt_ref[...])
    mag = jnp.exp(lr * dt)
    a_re = mag * jnp.cos(li * dt)
    a_im = mag * jnp.sin(li * dt)
    den = lr * lr + li * li
    g_re = ((a_re - 1.0) * lr + a_im * li) / den
    g_im = (a_im * lr - (a_re - 1.0) * li) / den
    br = bre_ref[...]
    bi = bim_ref[...]
    are_ref[...] = a_re
    aim_ref[...] = a_im
    bbre_ref[...] = g_re * br - g_im * bi
    bbim_ref[...] = g_re * bi + g_im * br


def _discretize(lam_re, lam_im, log_dt, b_re, b_im):
    col = lambda a: a.astype(F32).reshape(N_STATE, 1)
    ldt = jnp.broadcast_to(log_dt.astype(F32)[:, None], (N_SSM_GROUPS, SSM_STATE))
    outs = pl.pallas_call(
        _discretize_kernel,
        out_shape=[jax.ShapeDtypeStruct((N_STATE, 1), F32)] * 2
        + [jax.ShapeDtypeStruct((N_STATE, SSM_GROUP), F32)] * 2,
        name="discretize",
    )(col(lam_re), col(lam_im), col(ldt),
      b_re.astype(F32).reshape(N_STATE, SSM_GROUP), b_im.astype(F32).reshape(N_STATE, SSM_GROUP))
    a_re, a_im, bb_re, bb_im = outs
    shape3 = (N_SSM_GROUPS, SSM_STATE, SSM_GROUP)
    return a_re.reshape(1, N_STATE), a_im.reshape(1, N_STATE), bb_re.reshape(shape3), bb_im.reshape(shape3)


def _block_diag_weights(bb_re, bb_im, c_re, c_im):
    gh = N_SSM_GROUPS // N_PART
    eye = jnp.eye(gh, dtype=F32)
    bts, cts = [], []
    for part in range(N_PART):
        sl = slice(part * gh, (part + 1) * gh)
        bd_in = lambda b: jnp.einsum('gnc,gh->gchn', b[sl], eye).reshape(PART_CH, PART_STATE)
        bts.append(jnp.concatenate([bd_in(bb_re), bd_in(bb_im)], axis=1))
        bd_out = lambda c: jnp.einsum('gcn,gh->hngc', c[sl], eye).reshape(PART_STATE, PART_CH)
        cts.append(jnp.concatenate([bd_out(c_re.astype(F32)), -bd_out(c_im.astype(F32))], axis=0))
    return jnp.stack(bts).astype(BF16), jnp.stack(cts).astype(BF16)


def _trunk_in_kernel(x_ref, gain_ref, wa_ref, wkt_ref, wvt_ref, bt_ref, ct_ref, are_ref, aim_ref,
                     dskip_ref, wglu_ref, bglu_ref, h0re_ref, h0im_ref,
                     kt_ref, vt_ref, q_ref, za_ref, ms_ref, hre_ref, him_ref,
                     s_ref, *, nb, lt, pitch):
    r = nb * lt
    d_model = x_ref.shape[-1]
    tc = pl.program_id(1)

    @pl.when(tc == 0)
    def _():
        hre_ref[...] = h0re_ref[...]
        him_ref[...] = h0im_ref[...]

    x = x_ref[...].reshape(r, d_model)
    inv = lax.rsqrt(jnp.mean(x * x, axis=-1, keepdims=True) + RMS_EPS)
    xn = (x * inv * gain_ref[...]).astype(BF16)

    def proj(c):
        return jnp.dot(xn, wa_ref[:, c * 512:(c + 1) * 512], preferred_element_type=F32)

    nt_dims = (((1,), (1,)), ((), ()))
    ka, _, kl = kt_ref.shape

    def project_kv(w_ref, o_ref):
        t = lax.dot_general(w_ref[...], xn, nt_dims, preferred_element_type=F32)
        for a in range(ka):
            o_ref[a] = t[:, a * kl:(a + 1) * kl]

    def project_q_za():
        q_ref[...] = (proj(2) * ATTN_SCALE).astype(BF16).reshape(q_ref.shape)
        za_ref[...] = proj(3).reshape(za_ref.shape)

    between_stages = [lambda: (project_kv(wkt_ref, kt_ref), project_kv(wvt_ref, vt_ref)), project_q_za]
    u = proj(0)
    u_bf = u.astype(BF16)

    n_re = PART_SLABS // 2

    def slab_rows(slot, slab):
        if pitch == lt:
            return s_ref[slot, slab, 0:r, :]
        return jnp.concatenate([s_ref[slot, slab, b * pitch:b * pitch + lt, :] for b in range(nb)], axis=0)

    def scan_group(part, slot, g):
        row0 = g * (SUBLANES * pitch)
        bsl = pl.ds(g * SUBLANES if isinstance(g, int) else pl.multiple_of(g * SUBLANES, SUBLANES), SUBLANES)
        lane0 = part * PART_STATE
        lanes = [slice(lane0 + i * LANES, lane0 + (i + 1) * LANES) for i in range(n_re)]
        ar = [jnp.broadcast_to(are_ref[:, ls], (SUBLANES, LANES)) for ls in lanes]
        ai = [jnp.broadcast_to(aim_ref[:, ls], (SUBLANES, LANES)) for ls in lanes]
        hr = [hre_ref[bsl, ls] for ls in lanes]
        hi = [him_ref[bsl, ls] for ls in lanes]
        for t in range(lt):
            rows = pl.ds(row0 + t, SUBLANES, stride=pitch)
            for i in range(n_re):
                h_re = ar[i] * hr[i] - ai[i] * hi[i] + s_ref[slot, i, rows, :]
                h_im = ar[i] * hi[i] + ai[i] * hr[i] + s_ref[slot, n_re + i, rows, :]
                s_ref[slot, i, rows, :] = h_re
                s_ref[slot, n_re + i, rows, :] = h_im
                hr[i], hi[i] = h_re, h_im
        for i, ls in enumerate(lanes):
            hre_ref[bsl, ls] = hr[i]
            him_ref[bsl, ls] = hi[i]

    y_parts = []
    for part in range(N_PART):
        slot = part % N_SLOT
        u_part = u_bf[:, part * PART_CH:(part + 1) * PART_CH]
        per = CHUNK_COLS // LANES
        for c in range(PART_SLABS // per):
            bu = jnp.dot(u_part, bt_ref[part, :, c * CHUNK_COLS:(c + 1) * CHUNK_COLS],
                         preferred_element_type=F32)
            for i in range(per):
                piece = bu[:, i * LANES:(i + 1) * LANES]
                if pitch == lt:
                    s_ref[slot, c * per + i, 0:r, :] = piece
                else:
                    for b in range(nb):
                        s_ref[slot, c * per + i, b * pitch:b * pitch + lt, :] = piece[b * lt:(b + 1) * lt]
        if part < len(between_stages):
            between_stages[part]()
        if nb == SUBLANES:
            scan_group(part, slot, 0)
        else:
            def group_body(g, carry, part=part, slot=slot):
                scan_group(part, slot, g)
                return carry
            lax.fori_loop(0, nb // SUBLANES, group_body, 0)
        y_p = jnp.zeros((r, PART_CH), F32)
        for c in range(PART_SLABS // per):
            hcat = jnp.concatenate([slab_rows(slot, c * per + i) for i in range(per)], axis=1).astype(BF16)
            y_p = y_p + jnp.dot(hcat, ct_ref[part, c * CHUNK_COLS:(c + 1) * CHUNK_COLS, :],
                                preferred_element_type=F32)
        y_parts.append(y_p)

    y = jnp.concatenate(y_parts, axis=1) + dskip_ref[...] * u
    y = jax.nn.gelu(y)
    gate = jax.nn.sigmoid(jnp.dot(y.astype(BF16), wglu_ref[...], preferred_element_type=F32) + bglu_ref[...])
    zs = proj(1)
    ms_ref[...] = (y * gate * (zs * jax.nn.sigmoid(zs))).astype(BF16).reshape(ms_ref.shape)


def _trunk_in(x3, weights, h0_re, h0_im, *, nb, lt, pitch, x_block, kt_block, kt_shape):
    (gain, wa, wkt, wvt, bt, ct, a_re, a_im, dskip, wglu, bglu) = weights
    xa, xr, d_model = x3.shape
    n_seq = h0_re.shape[0]
    grid = (n_seq // nb, (xa * xr) // (n_seq * lt))
    n_tc = grid[1]
    row_spec = lambda w: pl.BlockSpec((x_block[0], x_block[1], w), lambda g, t: (g, t, 0))
    state_spec = pl.BlockSpec((nb, N_STATE), lambda g, t: (g, 0))
    kt_spec = pl.BlockSpec(kt_block, lambda g, t: (g, 0, t))
    del n_tc
    out_shape = [
        jax.ShapeDtypeStruct(kt_shape, F32), jax.ShapeDtypeStruct(kt_shape, F32),
        jax.ShapeDtypeStruct((xa, xr, ATTN_WIDTH), BF16),
        jax.ShapeDtypeStruct((xa, xr, ATTN_WIDTH), F32),
        jax.ShapeDtypeStruct((xa, xr, SSM_WIDTH), BF16),
        jax.ShapeDtypeStruct((n_seq, N_STATE), F32), jax.ShapeDtypeStruct((n_seq, N_STATE), F32),
    ]
    return pl.pallas_call(
        functools.partial(_trunk_in_kernel, nb=nb, lt=lt, pitch=pitch),
        grid=grid,
        in_specs=[row_spec(d_model), _const_spec(gain.shape), _const_spec(wa.shape),
                  _const_spec(wkt.shape), _const_spec(wvt.shape), _const_spec(bt.shape),
                  _const_spec(ct.shape), _const_spec(a_re.shape), _const_spec(a_im.shape),
                  _const_spec(dskip.shape), _const_spec(wglu.shape), _const_spec(bglu.shape),
                  state_spec, state_spec],
        out_specs=[kt_spec, kt_spec, row_spec(ATTN_WIDTH), row_spec(ATTN_WIDTH), row_spec(SSM_WIDTH),
                   state_spec, state_spec],
        out_shape=out_shape,
        scratch_shapes=[pltpu.VMEM((N_SLOT, PART_SLABS, nb * pitch, LANES), F32)],
        compiler_params=pltpu.CompilerParams(
            dimension_semantics=("parallel", "arbitrary"), vmem_limit_bytes=VMEM_LIMIT),
        name="trunk_in",
    )(x3, gain, wa, wkt, wvt, bt, ct, a_re, a_im, dskip, wglu, bglu, h0_re, h0_im)


def _prompt_attn_kernel(bias_ref, q_ref, kt_ref, vt_ref, za_ref, u_ref, o_ref,
                        ktb_ref, vtb_ref, qh_ref, acc_ref, carry_ref, *, tq, tk):
    qi = pl.program_id(1)
    n_blk = ktb_ref.shape[0]
    nt_dims = (((1,), (1,)), ((), ()))

    @pl.when(qi == 0)
    def _():
        aug_row = lax.broadcasted_iota(jnp.int32, (HEAD_DIM, tk), 0)
        zeros = jnp.zeros((HEAD_DIM, tk), BF16)
        for h in range(N_HEADS):
            hd = slice(h * HEAD_DIM, (h + 1) * HEAD_DIM)
            b = jnp.full((HEAD_DIM, tk), bias_ref[h], F32)
            b_hi = b.astype(BF16).astype(F32)
            b_mid = (b - b_hi).astype(BF16).astype(F32)
            b_lo = b - b_hi - b_mid
            aug = jnp.where(aug_row == 0, b_hi, jnp.where(aug_row == 1, b_mid,
                                                         jnp.where(aug_row == 2, b_lo, 0.0))).astype(BF16)
            for j in range(n_blk):
                keys = slice(j * tk, (j + 1) * tk)
                ktb_ref[j, h, 0:HEAD_DIM, :] = kt_ref[0, hd, keys].astype(BF16)
                ktb_ref[j, h, HEAD_DIM:, :] = aug
                vt = vt_ref[0, hd, keys].astype(BF16)
                vtb_ref[j, h, 0:HEAD_DIM, :] = vt if h % 2 == 0 else zeros
                vtb_ref[j, h, HEAD_DIM:, :] = zeros if h % 2 == 0 else vt

    ones = (lax.broadcasted_iota(jnp.int32, (tq, HEAD_DIM), 1) < 3).astype(F32).astype(BF16)
    for h in range(N_HEADS):
        qh_ref[h] = jnp.concatenate([q_ref[0, :, h * HEAD_DIM:(h + 1) * HEAD_DIM], ones], axis=1)

    row = lax.broadcasted_iota(jnp.int32, (tk, tk), 0)
    col = lax.broadcasted_iota(jnp.int32, (tk, tk), 1)
    visible = col < row
    u = u_ref[0:tk, :]

    def block(kb, rows, diagonal):
        zs, ls, css, ws, pvs = ({} for _ in range(5))

        def run(stage, h):
            if stage == 0:
                zs[h] = jnp.dot(qh_ref[h, rows, :], ktb_ref[kb, h], preferred_element_type=F32)
            elif stage == 1:
                l = _softplus(zs[h])
                ls[h] = jnp.where(visible, l, 0.0) if diagonal else l
            elif stage == 2:
                css[h] = _suffix_sums_bf16(ls[h], u)
            elif stage == 3:
                total = css[h][:, 0:1] + ls[h][:, 0:1]
                if diagonal:
                    ws[h] = jnp.where(visible, jnp.exp(zs[h] - ls[h] - css[h]), 0.0).astype(BF16)
                    carry_ref[h, rows, :] = total
                else:
                    carry = carry_ref[h, rows, :]
                    ws[h] = jnp.exp(zs[h] - ls[h] - css[h] - carry).astype(BF16)
                    carry_ref[h, rows, :] = carry + total
            else:
                pvs[h] = lax.dot_general(ws[h], vtb_ref[kb, h], nt_dims, preferred_element_type=F32)
                if h % 2 == 1:
                    pair = pvs[h - 1] + pvs[h]
                    if diagonal:
                        acc_ref[h // 2, rows, :] = pair
                    else:
                        acc_ref[h // 2, rows, :] += pair

        n_stage = 5
        for tick in range(N_HEADS + n_stage - 1):
            for stage in reversed(range(n_stage)):
                if 0 <= tick - stage < N_HEADS:
                    run(stage, tick - stage)

    n_sub = tq // tk
    first = qi * n_sub
    for s in range(n_sub):
        band = slice(s * tk, (s + 1) * tk)
        block(first + s, band, True)
        for kb_local in reversed(range(s)):
            block(first + kb_local, band, False)

    def body(j, c):
        block(first - 1 - j, slice(0, tq), False)
        return c
    lax.fori_loop(0, first, body, 0)

    za = za_ref[0]
    o = jnp.concatenate([acc_ref[p] for p in range(N_HEADS // 2)], axis=1)
    o_ref[0] = (o * (za * jax.nn.sigmoid(za))).astype(BF16)


def _prompt_attn(sb_bias, q, kt, vt, za, u):
    n_b, t, _ = q.shape
    tq, tk = ATTN_TQ, ATTN_TK
    q_spec = pl.BlockSpec((1, tq, ATTN_WIDTH), lambda b, i: (b, i, 0))
    kv_spec = pl.BlockSpec((1, ATTN_WIDTH, t), lambda b, i: (b, 0, 0))
    return pl.pallas_call(
        functools.partial(_prompt_attn_kernel, tq=tq, tk=tk),
        grid=(n_b, t // tq),
        in_specs=[pl.BlockSpec(memory_space=pltpu.SMEM), q_spec, kv_spec, kv_spec, q_spec,
                  _const_spec(u.shape)],
        out_specs=q_spec,
        out_shape=jax.ShapeDtypeStruct((n_b, t, ATTN_WIDTH), BF16),
        scratch_shapes=[pltpu.VMEM((t // tk, N_HEADS, 2 * HEAD_DIM, tk), BF16)] * 2
        + [pltpu.VMEM((N_HEADS, tq, 2 * HEAD_DIM), BF16), pltpu.VMEM((N_HEADS // 2, tq, 2 * HEAD_DIM), F32),
           pltpu.VMEM((N_HEADS, tq, 1), F32)],
        compiler_params=pltpu.CompilerParams(
            dimension_semantics=("parallel", "arbitrary"), vmem_limit_bytes=VMEM_LIMIT),
        name="prompt_attn",
    )(sb_bias, q, kt, vt, za, u)


def _sample_attn_kernel(pt_ref, q_ref, kn_ref, vn_ref, za_ref, bias_ref, u_ref, *rest, n_pages):
    del pt_ref
    k_pages = rest[:n_pages]
    v_pages = rest[n_pages:2 * n_pages]
    o_ref, acc_ref, carry_ref, qbd_ref = rest[2 * n_pages:]
    c = pl.program_id(1)
    n_q = q_ref.shape[1]
    rows = n_q * N_HEADS
    nt_dims = (((1,), (1,)), ((), ()))
    bias = bias_ref[...]

    @pl.when(c == 0)
    def _():
        q = q_ref[0].astype(F32)
        head_of_lane = lax.broadcasted_iota(jnp.int32, (N_HEADS, ATTN_WIDTH), 1) // HEAD_DIM
        own = head_of_lane == lax.broadcasted_iota(jnp.int32, (N_HEADS, ATTN_WIDTH), 0)
        qbd = jnp.concatenate(
            [jnp.where(own, jnp.broadcast_to(q[i:i + 1], (N_HEADS, ATTN_WIDTH)), 0.0) for i in range(n_q)],
            axis=0).astype(BF16)
        qbd_ref[...] = qbd
        pad = jnp.zeros((LANES - kn_ref.shape[1], ATTN_WIDTH), F32)
        kn = jnp.concatenate([kn_ref[0], pad], axis=0).astype(BF16)
        vn = jnp.concatenate([vn_ref[0], pad], axis=0).astype(BF16)
        z = lax.dot_general(qbd, kn, nt_dims, preferred_element_type=F32) + bias
        q_idx = lax.broadcasted_iota(jnp.int32, (rows, LANES), 0) // N_HEADS
        visible = lax.broadcasted_iota(jnp.int32, (rows, LANES), 1) < q_idx
        l = jnp.where(visible, _softplus(z), 0.0)
        l_wide = jnp.concatenate([l, jnp.zeros((rows, u_ref.shape[1] - LANES), F32)], axis=1)
        logw = z - l - _suffix_sums(l_wide, u_ref[...])[:, 0:LANES]
        w = jnp.where(visible, jnp.exp(logw), 0.0)
        acc_ref[...] = jnp.dot(w.astype(BF16), vn, preferred_element_type=F32)
        carry_ref[...] = jnp.sum(l, axis=1, keepdims=True)

    page_t = lambda ref: ref[...].reshape(ATTN_WIDTH, LANES).astype(BF16)
    kt = jnp.concatenate([page_t(r) for r in k_pages], axis=1)
    z = jnp.dot(qbd_ref[...], kt, preferred_element_type=F32) + bias
    l = _softplus(z)
    blk = u_ref.shape[1]
    n_blk = n_pages * LANES // blk
    l_blocks = [l[:, p * blk:(p + 1) * blk] for p in range(n_blk)]
    cs_all = _suffix_sums(jnp.concatenate(l_blocks, axis=0), u_ref[...])
    carry = carry_ref[...]
    w_blocks = [None] * n_blk
    for p in reversed(range(n_blk)):
        cs = cs_all[p * rows:(p + 1) * rows]
        zb = z[:, p * blk:(p + 1) * blk]
        w_blocks[p] = jnp.exp(zb - l_blocks[p] - cs - carry).astype(BF16)
        carry = carry + cs[:, 0:1] + l_blocks[p][:, 0:1]
    carry_ref[...] = carry
    vt = jnp.concatenate([page_t(r) for r in v_pages], axis=1)
    acc = acc_ref[...] + lax.dot_general(jnp.concatenate(w_blocks, axis=1), vt, nt_dims,
                                         preferred_element_type=F32)
    acc_ref[...] = acc

    @pl.when(c == pl.num_programs(1) - 1)
    def _():
        head_of_lane = lax.broadcasted_iota(jnp.int32, (rows, ATTN_WIDTH), 1) // HEAD_DIM
        own = head_of_lane == lax.broadcasted_iota(jnp.int32, (rows, ATTN_WIDTH), 0) % N_HEADS
        picked = jnp.where(own, acc, 0.0).reshape(n_q, N_HEADS, ATTN_WIDTH).sum(axis=1)
        za = za_ref[0]
        o_ref[0] = (picked * (za * jax.nn.sigmoid(za))).astype(BF16)


def _sample_attn(page_table, q, kn, vn, za, bias_rows, u, pool_kt, pool_vt):
    n_b, n_q, _ = q.shape
    n_logical = page_table.shape[1]
    npg = PAGES_PER_STEP
    n_chunks = n_logical // npg
    pt_flat = page_table.reshape(-1)
    per_b = lambda w: pl.BlockSpec((1, w, ATTN_WIDTH), lambda b, c, pt: (b, 0, 0))

    def page_spec(i):
        def index(b, c, pt):
            return (pt[b * n_logical + n_logical - npg * (c + 1) + i], 0, 0, 0)
        return pl.BlockSpec((None, N_HEADS, HEAD_DIM, LANES), index)

    rows = n_q * N_HEADS
    const = lambda shape: pl.BlockSpec(shape, lambda b, c, pt: (0,) * len(shape))
    grid_spec = pltpu.PrefetchScalarGridSpec(
        num_scalar_prefetch=1,
        grid=(n_b, n_chunks),
        in_specs=[per_b(n_q), per_b(kn.shape[1]), per_b(vn.shape[1]), per_b(n_q),
                  const(bias_rows.shape), const(u.shape)]
        + [page_spec(i) for i in range(npg)] * 2,
        out_specs=per_b(n_q),
        scratch_shapes=[pltpu.VMEM((rows, ATTN_WIDTH), F32), pltpu.VMEM((rows, 1), F32),
                        pltpu.VMEM((rows, ATTN_WIDTH), BF16)],
    )
    return pl.pallas_call(
        functools.partial(_sample_attn_kernel, n_pages=npg),
        grid_spec=grid_spec,
        out_shape=jax.ShapeDtypeStruct((n_b, n_q, ATTN_WIDTH), BF16),
        compiler_params=pltpu.CompilerParams(
            dimension_semantics=("parallel", "arbitrary"), vmem_limit_bytes=VMEM_LIMIT),
        name="sample_attn",
    )(pt_flat, q, kn, vn, za, bias_rows, u, *([pool_kt] * npg), *([pool_vt] * npg))


def _out_proj_kernel(x_ref, ms_ref, ma_ref, wo_ref, gain_ref, y_ref):
    h = (x_ref[...]
         + jnp.dot(ms_ref[...], wo_ref[0:SSM_WIDTH, :], preferred_element_type=F32)
         + jnp.dot(ma_ref[...], wo_ref[SSM_WIDTH:, :], preferred_element_type=F32))
    inv = lax.rsqrt(jnp.mean(h * h, axis=-1, keepdims=True) + RMS_EPS)
    y_ref[...] = h * inv * gain_ref[...]


def _out_proj(x2, ms, ma, wo, gain):
    n_rows, d_model = x2.shape
    tr = min(OUT_ROWS, n_rows)
    rows = lambda w: pl.BlockSpec((tr, w), lambda i: (i, 0))
    return pl.pallas_call(
        _out_proj_kernel,
        grid=(n_rows // tr,),
        in_specs=[rows(d_model), rows(SSM_WIDTH), rows(ATTN_WIDTH), _const_spec(wo.shape),
                  _const_spec(gain.shape)],
        out_specs=rows(d_model),
        out_shape=jax.ShapeDtypeStruct((n_rows, d_model), F32),
        compiler_params=pltpu.CompilerParams(
            dimension_semantics=("parallel",), vmem_limit_bytes=VMEM_LIMIT),
        name="out_proj",
    )(x2, ms, ma, wo, gain)


def kernel(x_prompt, x_sample, cache_k, cache_v, state_ssm_re, state_ssm_im, page_table, norm_gain, w_in, sb_bias, lambda_re, lambda_im, log_dt, b_re, b_im, c_re, c_im, d_skip, w_glu, b_glu, w_out, final_norm_gain):
    assert w_in.shape[0] == 1, "single-layer trunk only"
    n_b, t, d_model = x_prompt.shape
    n_s, n_q, _ = x_sample.shape
    assert (d_model, w_in.shape[2]) == (1024, 2 * SSM_WIDTH + 4 * ATTN_WIDTH)
    assert n_b % PROMPT_NB == 0 and t % PROMPT_LT == 0 and t % ATTN_TQ == 0 and n_s % SUBLANES == 0

    a_re, a_im, bb_re, bb_im = _discretize(lambda_re[0], lambda_im[0], log_dt[0], b_re[0], b_im[0])
    bt, ct = _block_diag_weights(bb_re, bb_im, c_re[0], c_im[0])
    w = w_in[0].astype(BF16)
    cut = 2 * SSM_WIDTH
    wa = jnp.concatenate([w[:, :cut + ATTN_WIDTH], w[:, cut + 3 * ATTN_WIDTH:]], axis=1)
    wkt = w[:, cut + ATTN_WIDTH:cut + 2 * ATTN_WIDTH].T
    wvt = w[:, cut + 2 * ATTN_WIDTH:cut + 3 * ATTN_WIDTH].T
    row = lambda a: a.astype(F32).reshape(1, -1)
    weights = (row(norm_gain[0]), wa, wkt, wvt, bt, ct, a_re, a_im, row(d_skip[0]),
               w_glu[0].astype(BF16), row(b_glu[0]))
    wo = w_out[0].astype(BF16)
    gain_f = row(final_norm_gain)
    tri = lax.broadcasted_iota(jnp.int32, (ATTN_TK, ATTN_TK), 0) > lax.broadcasted_iota(jnp.int32, (ATTN_TK, ATTN_TK), 1)
    u = jnp.concatenate([tri, tri], axis=0).astype(BF16)

    zeros = jnp.zeros((n_b, N_STATE), F32)
    pitch = PROMPT_LT + SUBLANES
    kt_p, vt_p, q_p, za_p, ms_p, hre_p, him_p = _trunk_in(
        x_prompt, weights, zeros, zeros, nb=PROMPT_NB, lt=PROMPT_LT, pitch=pitch,
        x_block=(PROMPT_NB, PROMPT_LT), kt_block=(PROMPT_NB, ATTN_WIDTH, PROMPT_LT),
        kt_shape=(n_b, ATTN_WIDTH, t))
    ma_p = _prompt_attn(sb_bias[0].astype(F32), q_p, kt_p, vt_p, za_p, u)
    y_prompt = _out_proj(x_prompt.reshape(n_b * t, d_model), ms_p.reshape(n_b * t, SSM_WIDTH),
                         ma_p.reshape(n_b * t, ATTN_WIDTH), wo, gain_f).reshape(n_b, t, d_model)
    heads_out = lambda a: a.reshape(1, n_b, N_HEADS, HEAD_DIM, t).transpose(0, 1, 4, 2, 3)
    state_out = lambda a, n: a.reshape(1, n, N_SSM_GROUPS, SSM_STATE)

    rows_s = n_s * n_q
    kt_s, vt_s, q_s, za_s, ms_s, hre_s, him_s = _trunk_in(
        x_sample.reshape(1, rows_s, d_model), weights,
        state_ssm_re[0].reshape(n_s, N_STATE).astype(F32), state_ssm_im[0].reshape(n_s, N_STATE).astype(F32),
        nb=n_s, lt=n_q, pitch=n_q, x_block=(1, rows_s), kt_block=(1, ATTN_WIDTH, rows_s),
        kt_shape=(1, ATTN_WIDTH, rows_s))
    new_rows = lambda a: a[0].T.reshape(n_s, n_q, ATTN_WIDTH)
    k_new, v_new = new_rows(kt_s), new_rows(vt_s)
    pad8 = lambda a: jnp.pad(a, ((0, 0), (0, SUBLANES - n_q), (0, 0)))
    bias_rows = jnp.tile(sb_bias[0].astype(F32), n_q).reshape(n_q * N_HEADS, 1)
    pool_t = lambda c: jnp.transpose(c[0], (0, 2, 3, 1))
    ma_s = _sample_attn(page_table, q_s.reshape(n_s, n_q, ATTN_WIDTH), pad8(k_new), pad8(v_new),
                        za_s.reshape(n_s, n_q, ATTN_WIDTH), bias_rows, u, pool_t(cache_k), pool_t(cache_v))
    y_sample = _out_proj(x_sample.reshape(rows_s, d_model), ms_s.reshape(rows_s, SSM_WIDTH),
                         ma_s.reshape(rows_s, ATTN_WIDTH), wo, gain_f).reshape(n_s, n_q, d_model)
    new_out = lambda a: a.reshape(1, n_s, n_q, N_HEADS, HEAD_DIM)

    return (y_prompt, y_sample,
            heads_out(kt_p), heads_out(vt_p), state_out(hre_p, n_b), state_out(him_p, n_b),
            new_out(k_new), new_out(v_new), state_out(hre_s, n_s), state_out(him_s, n_s))
```

```python
import functools
import math

import jax
import jax.numpy as jnp
from jax import lax
from jax.experimental import pallas as pl
from jax.experimental.pallas import tpu as pltpu

F32 = jnp.float32
BF16 = jnp.bfloat16

N_HEADS = 8
HEAD_DIM = 64
ATTN_WIDTH = N_HEADS * HEAD_DIM
SSM_WIDTH = 512
SSM_GROUP = 16
N_SSM_GROUPS = SSM_WIDTH // SSM_GROUP
SSM_STATE = 64
N_STATE = N_SSM_GROUPS * SSM_STATE
N_PART = 2
N_SLOT = 1
CHUNK_COLS = 512
PART_STATE = N_STATE // N_PART
PART_CH = SSM_WIDTH // N_PART
ATTN_SCALE = HEAD_DIM ** -0.5
RMS_EPS = 1e-6
LOG2E = math.log2(math.e)
HIDDEN_LOGIT = -1e30

LANES = 128
SUBLANES = 8
PART_SLABS = 2 * PART_STATE // LANES
VMEM_LIMIT = 56 * 1024 * 1024

PROMPT_NB = 8
PROMPT_LT = 128
ATTN_TQ = 512
ATTN_TK = 256
PAGES_PER_STEP = 32
OUT_ROWS = 2048


def _const_spec(shape):
    nd = len(shape)
    return pl.BlockSpec(shape, lambda *_: (0,) * nd, pipeline_mode=pl.Buffered(1))


def _softplus(z):
    return jnp.maximum(z, 0.0) + jnp.log(1.0 + jnp.exp2(jnp.abs(z) * (-LOG2E)))


def _suffix_sums(l, u2):
    l_hi = l.astype(BF16)
    l_lo = (l - l_hi.astype(F32)).astype(BF16)
    return jnp.dot(jnp.concatenate([l_hi, l_lo], axis=1), u2, preferred_element_type=F32)


def _suffix_sums_bf16(l, u):
    return jnp.dot(l.astype(BF16), u, preferred_element_type=F32)


def _discretize_kernel(lre_ref, lim_ref, ldt_ref, bre_ref, bim_ref,
                       are_ref, aim_ref, bbre_ref, bbim_ref):
    lr = lre_ref[...]
    li = lim_ref[...]
    dt = jnp.exp(ldt_ref[...])
    mag = jnp.exp(lr * dt)
    a_re = mag * jnp.cos(li * dt)
    a_im = mag * jnp.sin(li * dt)
    den = lr * lr + li * li
    g_re = ((a_re - 1.0) * lr + a_im * li) / den
    g_im = (a_im * lr - (a_re - 1.0) * li) / den
    br = bre_ref[...]
    bi = bim_ref[...]
    are_ref[...] = a_re
    aim_ref[...] = a_im
    bbre_ref[...] = g_re * br - g_im * bi
    bbim_ref[...] = g_re * bi + g_im * br


def _discretize(lam_re, lam_im, log_dt, b_re, b_im):
    col = lambda a: a.astype(F32).reshape(N_STATE, 1)
    ldt = jnp.broadcast_to(log_dt.astype(F32)[:, None], (N_SSM_GROUPS, SSM_STATE))
    outs = pl.pallas_call(
        _discretize_kernel,
        out_shape=[jax.ShapeDtypeStruct((N_STATE, 1), F32)] * 2
        + [jax.ShapeDtypeStruct((N_STATE, SSM_GROUP), F32)] * 2,
        name="discretize",
    )(col(lam_re), col(lam_im), col(ldt),
      b_re.astype(F32).reshape(N_STATE, SSM_GROUP), b_im.astype(F32).reshape(N_STATE, SSM_GROUP))
    a_re, a_im, bb_re, bb_im = outs
    shape3 = (N_SSM_GROUPS, SSM_STATE, SSM_GROUP)
    return a_re.reshape(1, N_STATE), a_im.reshape(1, N_STATE), bb_re.reshape(shape3), bb_im.reshape(shape3)


def _block_diag_weights(bb_re, bb_im, c_re, c_im):
    gh = N_SSM_GROUPS // N_PART
    eye = jnp.eye(gh, dtype=F32)
    bts, cts = [], []
    for part in range(N_PART):
        sl = slice(part * gh, (part + 1) * gh)
        bd_in = lambda b: jnp.einsum('gnc,gh->gchn', b[sl], eye).reshape(PART_CH, PART_STATE)
        bts.append(jnp.concatenate([bd_in(bb_re), bd_in(bb_im)], axis=1))
        bd_out = lambda c: jnp.einsum('gcn,gh->hngc', c[sl], eye).reshape(PART_STATE, PART_CH)
        cts.append(jnp.concatenate([bd_out(c_re.astype(F32)), -bd_out(c_im.astype(F32))], axis=0))
    return jnp.stack(bts).astype(BF16), jnp.stack(cts).astype(BF16)


def _trunk_in_kernel(x_ref, gain_ref, wa_ref, wkt_ref, wvt_ref, bt_ref, ct_ref, are_ref, aim_ref,
                     dskip_ref, wglu_ref, bglu_ref, h0re_ref, h0im_ref,
                     kt_ref, vt_ref, q_ref, za_ref, ms_ref, hre_ref, him_ref,
                     s_ref, *, nb, lt, pitch):
    r = nb * lt
    d_model = x_ref.shape[-1]
    tc = pl.program_id(1)

    @pl.when(tc == 0)
    def _():
        hre_ref[...] = h0re_ref[...]
        him_ref[...] = h0im_ref[...]

    x = x_ref[...].reshape(r, d_model)
    inv = lax.rsqrt(jnp.mean(x * x, axis=-1, keepdims=True) + RMS_EPS)
    xn = (x * inv * gain_ref[...]).astype(BF16)

    def proj(c):
        return jnp.dot(xn, wa_ref[:, c * 512:(c + 1) * 512], preferred_element_type=F32)

    nt_dims = (((1,), (1,)), ((), ()))
    ka, _, kl = kt_ref.shape

    def project_kv(w_ref, o_ref):
        t = lax.dot_general(w_ref[...], xn, nt_dims, preferred_element_type=F32)
        for a in range(ka):
            o_ref[a] = t[:, a * kl:(a + 1) * kl]

    def project_q_za():
        q_ref[...] = (proj(2) * ATTN_SCALE).astype(BF16).reshape(q_ref.shape)
        za_ref[...] = proj(3).reshape(za_ref.shape)

    between_stages = [lambda: (project_kv(wkt_ref, kt_ref), project_kv(wvt_ref, vt_ref)), project_q_za]
    u = proj(0)
    u_bf = u.astype(BF16)

    n_re = PART_SLABS // 2

    def slab_rows(slot, slab):
        if pitch == lt:
            return s_ref[slot, slab, 0:r, :]
        return jnp.concatenate([s_ref[slot, slab, b * pitch:b * pitch + lt, :] for b in range(nb)], axis=0)

    def scan_group(part, slot, g):
        row0 = g * (SUBLANES * pitch)
        bsl = pl.ds(g * SUBLANES if isinstance(g, int) else pl.multiple_of(g * SUBLANES, SUBLANES), SUBLANES)
        lane0 = part * PART_STATE
        lanes = [slice(lane0 + i * LANES, lane0 + (i + 1) * LANES) for i in range(n_re)]
        ar = [jnp.broadcast_to(are_ref[:, ls], (SUBLANES, LANES)) for ls in lanes]
        ai = [jnp.broadcast_to(aim_ref[:, ls], (SUBLANES, LANES)) for ls in lanes]
        hr = [hre_ref[bsl, ls] for ls in lanes]
        hi = [him_ref[bsl, ls] for ls in lanes]
        for t in range(lt):
            rows = pl.ds(row0 + t, SUBLANES, stride=pitch)
            for i in range(n_re):
                h_re = ar[i] * hr[i] - ai[i] * hi[i] + s_ref[slot, i, rows, :]
                h_im = ar[i] * hi[i] + ai[i] * hr[i] + s_ref[slot, n_re + i, rows, :]
                s_ref[slot, i, rows, :] = h_re
                s_ref[slot, n_re + i, rows, :] = h_im
                hr[i], hi[i] = h_re, h_im
        for i, ls in enumerate(lanes):
            hre_ref[bsl, ls] = hr[i]
            him_ref[bsl, ls] = hi[i]

    y_parts = []
    for part in range(N_PART):
        slot = part % N_SLOT
        u_part = u_bf[:, part * PART_CH:(part + 1) * PART_CH]
        per = CHUNK_COLS // LANES
        for c in range(PART_SLABS // per):
            bu = jnp.dot(u_part, bt_ref[part, :, c * CHUNK_COLS:(c + 1) * CHUNK_COLS],
                         preferred_element_type=F32)
            for i in range(per):
                piece = bu[:, i * LANES:(i + 1) * LANES]
                if pitch == lt:
                    s_ref[slot, c * per + i, 0:r, :] = piece
                else:
                    for b in range(nb):
                        s_ref[slot, c * per + i, b * pitch:b * pitch + lt, :] = piece[b * lt:(b + 1) * lt]
        if part < len(between_stages):
            between_stages[part]()
        if nb == SUBLANES:
            scan_group(part, slot, 0)
        else:
            def group_body(g, carry, part=part, slot=slot):
                scan_group(part, slot, g)
                return carry
            lax.fori_loop(0, nb // SUBLANES, group_body, 0)
        y_p = jnp.zeros((r, PART_CH), F32)
        for c in range(PART_SLABS // per):
            hcat = jnp.concatenate([slab_rows(slot, c * per + i) for i in range(per)], axis=1).astype(BF16)
            y_p = y_p + jnp.dot(hcat, ct_ref[part, c * CHUNK_COLS:(c + 1) * CHUNK_COLS, :],
                                preferred_element_type=F32)
        y_parts.append(y_p)

    y = jnp.concatenate(y_parts, axis=1) + dskip_ref[...] * u
    y = jax.nn.gelu(y)
    gate = jax.nn.sigmoid(jnp.dot(y.astype(BF16), wglu_ref[...], preferred_element_type=F32) + bglu_ref[...])
    zs = proj(1)
    ms_ref[...] = (y * gate * (zs * jax.nn.sigmoid(zs))).astype(BF16).reshape(ms_ref.shape)


def _trunk_in(x3, weights, h0_re, h0_im, *, nb, lt, pitch, x_block, kt_block, kt_shape):
    (gain, wa, wkt, wvt, bt, ct, a_re, a_im, dskip, wglu, bglu) = weights
    xa, xr, d_model = x3.shape
    n_seq = h0_re.shape[0]
    grid = (n_seq // nb, (xa * xr) // (n_seq * lt))
    n_tc = grid[1]
    row_spec = lambda w: pl.BlockSpec((x_block[0], x_block[1], w), lambda g, t: (g, t, 0))
    state_spec = pl.BlockSpec((nb, N_STATE), lambda g, t: (g, 0))
    kt_spec = pl.BlockSpec(kt_block, lambda g, t: (g, 0, t))
    del n_tc
    out_shape = [
        jax.ShapeDtypeStruct(kt_shape, F32), jax.ShapeDtypeStruct(kt_shape, F32),
        jax.ShapeDtypeStruct((xa, xr, ATTN_WIDTH), BF16),
        jax.ShapeDtypeStruct((xa, xr, ATTN_WIDTH), F32),
        jax.ShapeDtypeStruct((xa, xr, SSM_WIDTH), BF16),
        jax.ShapeDtypeStruct((n_seq, N_STATE), F32), jax.ShapeDtypeStruct((n_seq, N_STATE), F32),
    ]
    return pl.pallas_call(
        functools.partial(_trunk_in_kernel, nb=nb, lt=lt, pitch=pitch),
        grid=grid,
        in_specs=[row_spec(d_model), _const_spec(gain.shape), _const_spec(wa.shape),
                  _const_spec(wkt.shape), _const_spec(wvt.shape), _const_spec(bt.shape),
                  _const_spec(ct.shape), _const_spec(a_re.shape), _const_spec(a_im.shape),
                  _const_spec(dskip.shape), _const_spec(wglu.shape), _const_spec(bglu.shape),
                  state_spec, state_spec],
        out_specs=[kt_spec, kt_spec, row_spec(ATTN_WIDTH), row_spec(ATTN_WIDTH), row_spec(SSM_WIDTH),
                   state_spec, state_spec],
        out_shape=out_shape,
        scratch_shapes=[pltpu.VMEM((N_SLOT, PART_SLABS, nb * pitch, LANES), F32)],
        compiler_params=pltpu.CompilerParams(
            dimension_semantics=("parallel", "arbitrary"), vmem_limit_bytes=VMEM_LIMIT),
        name="trunk_in",
    )(x3, gain, wa, wkt, wvt, bt, ct, a_re, a_im, dskip, wglu, bglu, h0_re, h0_im)


def _prompt_attn_kernel(bias_ref, q_ref, kt_ref, vt_ref, za_ref, u_ref, o_ref,
                        ktb_ref, vtb_ref, qh_ref, acc_ref, carry_ref, *, tq, tk):
    qi = pl.program_id(1)
    n_blk = ktb_ref.shape[0]
    nt_dims = (((1,), (1,)), ((), ()))

    @pl.when(qi == 0)
    def _():
        aug_row = lax.broadcasted_iota(jnp.int32, (HEAD_DIM, tk), 0)
        zeros = jnp.zeros((HEAD_DIM, tk), BF16)
        for h in range(N_HEADS):
            hd = slice(h * HEAD_DIM, (h + 1) * HEAD_DIM)
            b = jnp.full((HEAD_DIM, tk), bias_ref[h], F32)
            b_hi = b.astype(BF16).astype(F32)
            b_mid = (b - b_hi).astype(BF16).astype(F32)
            b_lo = b - b_hi - b_mid
            aug = jnp.where(aug_row == 0, b_hi, jnp.where(aug_row == 1, b_mid,
                                                         jnp.where(aug_row == 2, b_lo, 0.0))).astype(BF16)
            for j in range(n_blk):
                keys = slice(j * tk, (j + 1) * tk)
                ktb_ref[j, h, 0:HEAD_DIM, :] = kt_ref[0, hd, keys].astype(BF16)
                ktb_ref[j, h, HEAD_DIM:, :] = aug
                vt = vt_ref[0, hd, keys].astype(BF16)
                vtb_ref[j, h, 0:HEAD_DIM, :] = vt if h % 2 == 0 else zeros
                vtb_ref[j, h, HEAD_DIM:, :] = zeros if h % 2 == 0 else vt

    ones = (lax.broadcasted_iota(jnp.int32, (tq, HEAD_DIM), 1) < 3).astype(F32).astype(BF16)
    for h in range(N_HEADS):
        qh_ref[h] = jnp.concatenate([q_ref[0, :, h * HEAD_DIM:(h + 1) * HEAD_DIM], ones], axis=1)

    row = lax.broadcasted_iota(jnp.int32, (tk, tk), 0)
    col = lax.broadcasted_iota(jnp.int32, (tk, tk), 1)
    visible = col < row
    u = u_ref[0:tk, :]

    def block(kb, rows, diagonal):
        zs, ls, css, ws, pvs = ({} for _ in range(5))

        def run(stage, h):
            if stage == 0:
                z = jnp.dot(qh_ref[h, rows, :], ktb_ref[kb, h], preferred_element_type=F32)
                zs[h] = jnp.where(visible, z, HIDDEN_LOGIT) if diagonal else z
            elif stage == 1:
                ls[h] = _softplus(zs[h])
            elif stage == 2:
                css[h] = _suffix_sums_bf16(ls[h], u)
            elif stage == 3:
                total = css[h][:, 0:1] + ls[h][:, 0:1]
                if diagonal:
                    ws[h] = jnp.exp(zs[h] - ls[h] - css[h]).astype(BF16)
                    carry_ref[h, rows, :] = total
                else:
                    carry = carry_ref[h, rows, :]
                    ws[h] = jnp.exp(zs[h] - ls[h] - css[h] - carry).astype(BF16)
                    carry_ref[h, rows, :] = carry + total
            else:
                pvs[h] = lax.dot_general(ws[h], vtb_ref[kb, h], nt_dims, preferred_element_type=F32)
                if h % 2 == 1:
                    pair = pvs[h - 1] + pvs[h]
                    if diagonal:
                        acc_ref[h // 2, rows, :] = pair
                    else:
                        acc_ref[h // 2, rows, :] += pair

        n_stage = 5
        for tick in range(N_HEADS + n_stage - 1):
            for stage in reversed(range(n_stage)):
                if 0 <= tick - stage < N_HEADS:
                    run(stage, tick - stage)

    n_sub = tq // tk
    first = qi * n_sub
    for s in range(n_sub):
        band = slice(s * tk, (s + 1) * tk)
        block(first + s, band, True)
        for kb_local in reversed(range(s)):
            block(first + kb_local, band, False)

    def body(j, c):
        block(first - 1 - j, slice(0, tq), False)
        return c
    lax.fori_loop(0, first, body, 0)

    za = za_ref[0]
    o = jnp.concatenate([acc_ref[p] for p in range(N_HEADS // 2)], axis=1)
    o_ref[0] = (o * (za * jax.nn.sigmoid(za))).astype(BF16)


def _prompt_attn(sb_bias, q, kt, vt, za, u):
    n_b, t, _ = q.shape
    tq, tk = ATTN_TQ, ATTN_TK
    q_spec = pl.BlockSpec((1, tq, ATTN_WIDTH), lambda b, i: (b, i, 0))
    kv_spec = pl.BlockSpec((1, ATTN_WIDTH, t), lambda b, i: (b, 0, 0))
    return pl.pallas_call(
        functools.partial(_prompt_attn_kernel, tq=tq, tk=tk),
        grid=(n_b, t // tq),
        in_specs=[pl.BlockSpec(memory_space=pltpu.SMEM), q_spec, kv_spec, kv_spec, q_spec,
                  _const_spec(u.shape)],
        out_specs=q_spec,
        out_shape=jax.ShapeDtypeStruct((n_b, t, ATTN_WIDTH), BF16),
        scratch_shapes=[pltpu.VMEM((t // tk, N_HEADS, 2 * HEAD_DIM, tk), BF16)] * 2
        + [pltpu.VMEM((N_HEADS, tq, 2 * HEAD_DIM), BF16), pltpu.VMEM((N_HEADS // 2, tq, 2 * HEAD_DIM), F32),
           pltpu.VMEM((N_HEADS, tq, 1), F32)],
        compiler_params=pltpu.CompilerParams(
            dimension_semantics=("parallel", "arbitrary"), vmem_limit_bytes=VMEM_LIMIT),
        name="prompt_attn",
    )(sb_bias, q, kt, vt, za, u)


def _sample_attn_kernel(pt_ref, q_ref, kn_ref, vn_ref, za_ref, bias_ref, u_ref, *rest, n_pages):
    del pt_ref
    k_pages = rest[:n_pages]
    v_pages = rest[n_pages:2 * n_pages]
    o_ref, acc_ref, carry_ref, qbd_ref = rest[2 * n_pages:]
    c = pl.program_id(1)
    n_q = q_ref.shape[1]
    rows = n_q * N_HEADS
    nt_dims = (((1,), (1,)), ((), ()))
    bias = bias_ref[...]

    @pl.when(c == 0)
    def _():
        q = q_ref[0].astype(F32)
        head_of_lane = lax.broadcasted_iota(jnp.int32, (N_HEADS, ATTN_WIDTH), 1) // HEAD_DIM
        own = head_of_lane == lax.broadcasted_iota(jnp.int32, (N_HEADS, ATTN_WIDTH), 0)
        qbd = jnp.concatenate(
            [jnp.where(own, jnp.broadcast_to(q[i:i + 1], (N_HEADS, ATTN_WIDTH)), 0.0) for i in range(n_q)],
            axis=0).astype(BF16)
        qbd_ref[...] = qbd
        pad = jnp.zeros((LANES - kn_ref.shape[1], ATTN_WIDTH), F32)
        kn = jnp.concatenate([kn_ref[0], pad], axis=0).astype(BF16)
        vn = jnp.concatenate([vn_ref[0], pad], axis=0).astype(BF16)
        z = lax.dot_general(qbd, kn, nt_dims, preferred_element_type=F32) + bias
        q_idx = lax.broadcasted_iota(jnp.int32, (rows, LANES), 0) // N_HEADS
        visible = lax.broadcasted_iota(jnp.int32, (rows, LANES), 1) < q_idx
        l = jnp.where(visible, _softplus(z), 0.0)
        l_wide = jnp.concatenate([l, jnp.zeros((rows, u_ref.shape[1] - LANES), F32)], axis=1)
        logw = z - l - _suffix_sums(l_wide, u_ref[...])[:, 0:LANES]
        w = jnp.where(visible, jnp.exp(logw), 0.0)
        acc_ref[...] = jnp.dot(w.astype(BF16), vn, preferred_element_type=F32)
        carry_ref[...] = jnp.sum(l, axis=1, keepdims=True)

    page_t = lambda ref: ref[...].reshape(ATTN_WIDTH, LANES).astype(BF16)
    kt = jnp.concatenate([page_t(r) for r in k_pages], axis=1)
    z = jnp.dot(qbd_ref[...], kt, preferred_element_type=F32) + bias
    l = _softplus(z)
    blk = u_ref.shape[1]
    n_blk = n_pages * LANES // blk
    l_blocks = [l[:, p * blk:(p + 1) * blk] for p in range(n_blk)]
    cs_all = _suffix_sums(jnp.concatenate(l_blocks, axis=0), u_ref[...])
    carry = carry_ref[...]
    w_blocks = [None] * n_blk
    for p in reversed(range(n_blk)):
        cs = cs_all[p * rows:(p + 1) * rows]
        zb = z[:, p * blk:(p + 1) * blk]
        w_blocks[p] = jnp.exp(zb - l_blocks[p] - cs - carry).astype(BF16)
        carry = carry + cs[:, 0:1] + l_blocks[p][:, 0:1]
    carry_ref[...] = carry
    vt = jnp.concatenate([page_t(r) for r in v_pages], axis=1)
    acc = acc_ref[...] + lax.dot_general(jnp.concatenate(w_blocks, axis=1), vt, nt_dims,
                                         preferred_element_type=F32)
    acc_ref[...] = acc

    @pl.when(c == pl.num_programs(1) - 1)
    def _():
        head_of_lane = lax.broadcasted_iota(jnp.int32, (rows, ATTN_WIDTH), 1) // HEAD_DIM
        own = head_of_lane == lax.broadcasted_iota(jnp.int32, (rows, ATTN_WIDTH), 0) % N_HEADS
        picked = jnp.where(own, acc, 0.0).reshape(n_q, N_HEADS, ATTN_WIDTH).sum(axis=1)
        za = za_ref[0]
        o_ref[0] = (picked * (za * jax.nn.sigmoid(za))).astype(BF16)


def _sample_attn(page_table, q, kn, vn, za, bias_rows, u, pool_kt, pool_vt):
    n_b, n_q, _ = q.shape
    n_logical = page_table.shape[1]
    npg = PAGES_PER_STEP
    n_chunks = n_logical // npg
    pt_flat = page_table.reshape(-1)
    per_b = lambda w: pl.BlockSpec((1, w, ATTN_WIDTH), lambda b, c, pt: (b, 0, 0))

    def page_spec(i):
        def index(b, c, pt):
            return (pt[b * n_logical + n_logical - npg * (c + 1) + i], 0, 0, 0)
        return pl.BlockSpec((None, N_HEADS, HEAD_DIM, LANES), index)

    rows = n_q * N_HEADS
    const = lambda shape: pl.BlockSpec(shape, lambda b, c, pt: (0,) * len(shape))
    grid_spec = pltpu.PrefetchScalarGridSpec(
        num_scalar_prefetch=1,
        grid=(n_b, n_chunks),
        in_specs=[per_b(n_q), per_b(kn.shape[1]), per_b(vn.shape[1]), per_b(n_q),
                  const(bias_rows.shape), const(u.shape)]
        + [page_spec(i) for i in range(npg)] * 2,
        out_specs=per_b(n_q),
        scratch_shapes=[pltpu.VMEM((rows, ATTN_WIDTH), F32), pltpu.VMEM((rows, 1), F32),
                        pltpu.VMEM((rows, ATTN_WIDTH), BF16)],
    )
    return pl.pallas_call(
        functools.partial(_sample_attn_kernel, n_pages=npg),
        grid_spec=grid_spec,
        out_shape=jax.ShapeDtypeStruct((n_b, n_q, ATTN_WIDTH), BF16),
        compiler_params=pltpu.CompilerParams(
            dimension_semantics=("parallel", "arbitrary"), vmem_limit_bytes=VMEM_LIMIT),
        name="sample_attn",
    )(pt_flat, q, kn, vn, za, bias_rows, u, *([pool_kt] * npg), *([pool_vt] * npg))


def _out_proj_kernel(x_ref, ms_ref, ma_ref, wo_ref, gain_ref, y_ref):
    h = (x_ref[...]
         + jnp.dot(ms_ref[...], wo_ref[0:SSM_WIDTH, :], preferred_element_type=F32)
         + jnp.dot(ma_ref[...], wo_ref[SSM_WIDTH:, :], preferred_element_type=F32))
    inv = lax.rsqrt(jnp.mean(h * h, axis=-1, keepdims=True) + RMS_EPS)
    y_ref[...] = h * inv * gain_ref[...]


def _out_proj(x2, ms, ma, wo, gain):
    n_rows, d_model = x2.shape
    tr = min(OUT_ROWS, n_rows)
    rows = lambda w: pl.BlockSpec((tr, w), lambda i: (i, 0))
    return pl.pallas_call(
        _out_proj_kernel,
        grid=(n_rows // tr,),
        in_specs=[rows(d_model), rows(SSM_WIDTH), rows(ATTN_WIDTH), _const_spec(wo.shape),
                  _const_spec(gain.shape)],
        out_specs=rows(d_model),
        out_shape=jax.ShapeDtypeStruct((n_rows, d_model), F32),
        compiler_params=pltpu.CompilerParams(
            dimension_semantics=("parallel",), vmem_limit_bytes=VMEM_LIMIT),
        name="out_proj",
    )(x2, ms, ma, wo, gain)


def kernel(x_prompt, x_sample, cache_k, cache_v, state_ssm_re, state_ssm_im, page_table, norm_gain, w_in, sb_bias, lambda_re, lambda_im, log_dt, b_re, b_im, c_re, c_im, d_skip, w_glu, b_glu, w_out, final_norm_gain):
    assert w_in.shape[0] == 1, "single-layer trunk only"
    n_b, t, d_model = x_prompt.shape
    n_s, n_q, _ = x_sample.shape
    assert (d_model, w_in.shape[2]) == (1024, 2 * SSM_WIDTH + 4 * ATTN_WIDTH)
    assert n_b % PROMPT_NB == 0 and t % PROMPT_LT == 0 and t % ATTN_TQ == 0 and n_s % SUBLANES == 0

    a_re, a_im, bb_re, bb_im = _discretize(lambda_re[0], lambda_im[0], log_dt[0], b_re[0], b_im[0])
    bt, ct = _block_diag_weights(bb_re, bb_im, c_re[0], c_im[0])
    w = w_in[0].astype(BF16)
    cut = 2 * SSM_WIDTH
    wa = jnp.concatenate([w[:, :cut + ATTN_WIDTH], w[:, cut + 3 * ATTN_WIDTH:]], axis=1)
    wkt = w[:, cut + ATTN_WIDTH:cut + 2 * ATTN_WIDTH].T
    wvt = w[:, cut + 2 * ATTN_WIDTH:cut + 3 * ATTN_WIDTH].T
    row = lambda a: a.astype(F32).reshape(1, -1)
    weights = (row(norm_gain[0]), wa, wkt, wvt, bt, ct, a_re, a_im, row(d_skip[0]),
               w_glu[0].astype(BF16), row(b_glu[0]))
    wo = w_out[0].astype(BF16)
    gain_f = row(final_norm_gain)
    tri = lax.broadcasted_iota(jnp.int32, (ATTN_TK, ATTN_TK), 0) > lax.broadcasted_iota(jnp.int32, (ATTN_TK, ATTN_TK), 1)
    u = jnp.concatenate([tri, tri], axis=0).astype(BF16)

    zeros = jnp.zeros((n_b, N_STATE), F32)
    pitch = PROMPT_LT + SUBLANES
    kt_p, vt_p, q_p, za_p, ms_p, hre_p, him_p = _trunk_in(
        x_prompt, weights, zeros, zeros, nb=PROMPT_NB, lt=PROMPT_LT, pitch=pitch,
        x_block=(PROMPT_NB, PROMPT_LT), kt_block=(PROMPT_NB, ATTN_WIDTH, PROMPT_LT),
        kt_shape=(n_b, ATTN_WIDTH, t))
    ma_p = _prompt_attn(sb_bias[0].astype(F32), q_p, kt_p, vt_p, za_p, u)
    y_prompt = _out_proj(x_prompt.reshape(n_b * t, d_model), ms_p.reshape(n_b * t, SSM_WIDTH),
                         ma_p.reshape(n_b * t, ATTN_WIDTH), wo, gain_f).reshape(n_b, t, d_model)
    heads_out = lambda a: a.reshape(1, n_b, N_HEADS, HEAD_DIM, t).transpose(0, 1, 4, 2, 3)
    state_out = lambda a, n: a.reshape(1, n, N_SSM_GROUPS, SSM_STATE)

    rows_s = n_s * n_q
    kt_s, vt_s, q_s, za_s, ms_s, hre_s, him_s = _trunk_in(
        x_sample.reshape(1, rows_s, d_model), weights,
        state_ssm_re[0].reshape(n_s, N_STATE).astype(F32), state_ssm_im[0].reshape(n_s, N_STATE).astype(F32),
        nb=n_s, lt=n_q, pitch=n_q, x_block=(1, rows_s), kt_block=(1, ATTN_WIDTH, rows_s),
        kt_shape=(1, ATTN_WIDTH, rows_s))
    new_rows = lambda a: a[0].T.reshape(n_s, n_q, ATTN_WIDTH)
    k_new, v_new = new_rows(kt_s), new_rows(vt_s)
    pad8 = lambda a: jnp.pad(a, ((0, 0), (0, SUBLANES - n_q), (0, 0)))
    bias_rows = jnp.tile(sb_bias[0].astype(F32), n_q).reshape(n_q * N_HEADS, 1)
    pool_t = lambda c: jnp.transpose(c[0], (0, 2, 3, 1))
    ma_s = _sample_attn(page_table, q_s.reshape(n_s, n_q, ATTN_WIDTH), pad8(k_new), pad8(v_new),
                        za_s.reshape(n_s, n_q, ATTN_WIDTH), bias_rows, u, pool_t(cache_k), pool_t(cache_v))
    y_sample = _out_proj(x_sample.reshape(rows_s, d_model), ms_s.reshape(rows_s, SSM_WIDTH),
                         ma_s.reshape(rows_s, ATTN_WIDTH), wo, gain_f).reshape(n_s, n_q, d_model)
    new_out = lambda a: a.reshape(1, n_s, n_q, N_HEADS, HEAD_DIM)

    return (y_prompt, y_sample,
            heads_out(kt_p), heads_out(vt_p), state_out(hre_p, n_b), state_out(him_p, n_b),
            new_out(k_new), new_out(v_new), state_out(hre_s, n_s), state_out(him_s, n_s))
```
